```python
import math
import jax, jax.numpy as jnp
from jax import lax
import numpy as np

D_MODEL = 2048
BATCH = 4
SEQ = 2048
DEPTH = 1

MEM_LEN = 256
EPS = 1e-6
ROPE_THETA = 10000.0
MLA_HEADS = 8
MLA_Q_RANK = 512
MLA_KV_RANK = 256
MLA_NOPE = 128
MLA_ROPE = 64
MLA_V = 128
ATTN_Q_BLOCK = 128
MOBA_HEADS = 8
MOBA_HEAD_DIM = 128
MOBA_BLOCK = 256
MOBA_TOPK = 3
MOBA_Q_CHUNK = 16
CROSS_HEADS = 4
CROSS_HEAD_DIM = D_MODEL // CROSS_HEADS
PEER_HEADS = 8
PEER_NKEYS = 128
PEER_N_EXPERTS = PEER_NKEYS * PEER_NKEYS
PEER_DKEY = 256
PEER_TOPK = 16
PEER_TOKEN_CHUNK = 128

MLA_OUT = MLA_HEADS * MLA_V
MOBA_WIDTH = MOBA_HEADS * MOBA_HEAD_DIM
MIX_WIDTH = MLA_OUT + MOBA_WIDTH
IN_COLS = MLA_Q_RANK + MLA_KV_RANK + MLA_ROPE + 3 * MOBA_WIDTH
IN_SPLITS = (MLA_Q_RANK,
             MLA_Q_RANK + MLA_KV_RANK,
             MLA_Q_RANK + MLA_KV_RANK + MLA_ROPE,
             MLA_Q_RANK + MLA_KV_RANK + MLA_ROPE + MOBA_WIDTH,
             MLA_Q_RANK + MLA_KV_RANK + MLA_ROPE + 2 * MOBA_WIDTH)

kernel_name = 'hybrid_mla_moba_peer_layer'


def rms_norm(x, g):
    xf = x.astype(jnp.float32)
    y = xf * lax.rsqrt(jnp.mean(xf * xf, axis=-1, keepdims=True) + EPS)
    return (y * g.astype(jnp.float32)).astype(x.dtype)


def rope(x, positions):
    dim = x.shape[-1]
    half = dim // 2
    inv_freq = jnp.exp(jnp.arange(half, dtype=jnp.float32) * (-2.0 * math.log(ROPE_THETA) / dim))
    ang = positions.astype(jnp.float32)[:, :, None] * inv_freq
    cos = jnp.cos(ang)[:, :, None, :]
    sin = jnp.sin(ang)[:, :, None, :]
    xf = x.astype(jnp.float32)
    x1, x2 = xf[..., :half], xf[..., half:]
    return jnp.concatenate([x1 * cos - x2 * sin, x2 * cos + x1 * sin], axis=-1).astype(x.dtype)


def causal_block_attention(q, k, v, scale):
    B, H, S, _ = q.shape
    dv = v.shape[-1]
    kpos = jnp.arange(S)

    def one(i):
        start = i * ATTN_Q_BLOCK
        qs = lax.dynamic_slice_in_dim(q, start, ATTN_Q_BLOCK, axis=2)
        s = jnp.einsum('bhqd,bhkd->bhqk', qs, k).astype(jnp.float32) * scale
        qpos = start + jnp.arange(ATTN_Q_BLOCK)
        s = jnp.where(kpos[None, :] <= qpos[:, None], s, -jnp.inf)
        p = jax.nn.softmax(s, axis=-1).astype(v.dtype)
        return jnp.einsum('bhqk,bhkd->bhqd', p, v)

    o = lax.map(one, jnp.arange(S // ATTN_Q_BLOCK))
    return o.transpose(1, 0, 3, 2, 4).reshape(B, S, H * dv)


def moba_attention(q, k, v):
    B, H, S, Dh = q.shape
    nb = -(-S // MOBA_BLOCK)
    pad = nb * MOBA_BLOCK - S
    k_pad = jnp.pad(k, ((0, 0), (0, 0), (0, pad), (0, 0)))
    v_pad = jnp.pad(v, ((0, 0), (0, 0), (0, pad), (0, 0)))
    kb = k_pad.reshape(B, H, nb, MOBA_BLOCK, Dh)
    vb = v_pad.reshape(B, H, nb, MOBA_BLOCK, Dh)
    k_eff = min(MOBA_TOPK, nb - 1)
    scale = Dh ** -0.5
    own_blk = jnp.arange(S) // MOBA_BLOCK
    if k_eff > 0:
        kbar = jnp.mean(kb.astype(jnp.float32), axis=3)
        gate = jnp.einsum('bhsd,bhnd->bhsn', q.astype(jnp.float32), kbar)
        past = jnp.arange(nb)[None, :] < own_blk[:, None]
        gate = jnp.where(past, gate, -jnp.inf)
        _, sel_idx = lax.top_k(gate, k_eff)
        sel_valid = sel_idx < own_blk[:, None]
    bi = jnp.arange(B)[:, None, None, None]
    hi = jnp.arange(H)[None, :, None, None]

    def one(ci):
        start = ci * MOBA_Q_CHUNK
        qc = lax.dynamic_slice_in_dim(q, start, MOBA_Q_CHUNK, axis=2)
        qpos = start + jnp.arange(MOBA_Q_CHUNK)
        ob = start // MOBA_BLOCK
        k_own = lax.dynamic_slice_in_dim(k_pad, ob * MOBA_BLOCK, MOBA_BLOCK, axis=2)
        v_own = lax.dynamic_slice_in_dim(v_pad, ob * MOBA_BLOCK, MOBA_BLOCK, axis=2)
        kpos = ob * MOBA_BLOCK + jnp.arange(MOBA_BLOCK)
        s_own = jnp.einsum('bhqd,bhkd->bhqk', qc, k_own).astype(jnp.float32) * scale
        s_own = jnp.where(kpos[None, :] <= qpos[:, None], s_own, -jnp.inf)
        if k_eff == 0:
            p = jax.nn.softmax(s_own, axis=-1).astype(v.dtype)
            return jnp.einsum('bhqk,bhkd->bhqd', p, v_own)
        idx = lax.dynamic_slice_in_dim(sel_idx, start, MOBA_Q_CHUNK, axis=2)
        ok = lax.dynamic_slice_in_dim(sel_valid, start, MOBA_Q_CHUNK, axis=2)
        k_sel = kb[bi, hi, idx]
        v_sel = vb[bi, hi, idx]
        s_sel = jnp.einsum('bhqd,bhqjkd->bhqjk', qc, k_sel).astype(jnp.float32) * scale
        s_sel = jnp.where(ok[..., None], s_sel, -jnp.inf).reshape(B, H, MOBA_Q_CHUNK, k_eff * MOBA_BLOCK)
        p = jax.nn.softmax(jnp.concatenate([s_sel, s_own], axis=-1), axis=-1).astype(v.dtype)
        p_sel = p[..., :k_eff * MOBA_BLOCK].reshape(B, H, MOBA_Q_CHUNK, k_eff, MOBA_BLOCK)
        p_own = p[..., k_eff * MOBA_BLOCK:]
        return (jnp.einsum('bhqjk,bhqjkd->bhqd', p_sel, v_sel)
                + jnp.einsum('bhqk,bhkd->bhqd', p_own, v_own))

    o = lax.map(one, jnp.arange(S // MOBA_Q_CHUNK))
    return o.transpose(1, 0, 3, 2, 4).reshape(B, S, H * Dh)


def memory_cross_attention(hn, mn, w_cq, w_ck, w_cv, w_co):
    B, S, D = hn.shape
    M = mn.shape[1]
    q = (hn @ w_cq).reshape(B, S, CROSS_HEADS, CROSS_HEAD_DIM)
    k = (mn @ w_ck).reshape(B, M, CROSS_HEADS, CROSS_HEAD_DIM)
    v = (mn @ w_cv).reshape(B, M, CROSS_HEADS, CROSS_HEAD_DIM)
    s = jnp.einsum('bshd,bmhd->bhsm', q, k).astype(jnp.float32) * (CROSS_HEAD_DIM ** -0.5)
    p = jax.nn.softmax(s, axis=-1).astype(v.dtype)
    o = jnp.einsum('bhsm,bmhd->bshd', p, v).reshape(B, S, D)
    return o @ w_co


def peer(xn, w_pq, sub_keys1, sub_keys2, expert_u, expert_v):
    B, S, D = xn.shape
    T = B * S
    xt = xn.reshape(T, D)
    q = (xt @ w_pq).reshape(T, PEER_HEADS, 2, PEER_DKEY // 2)
    s1 = jnp.einsum('thd,nd->thn', q[:, :, 0], sub_keys1).astype(jnp.float32)
    s2 = jnp.einsum('thd,nd->thn', q[:, :, 1], sub_keys2).astype(jnp.float32)
    v1, i1 = lax.top_k(s1, PEER_TOPK)
    v2, i2 = lax.top_k(s2, PEER_TOPK)
    cand = (v1[..., :, None] + v2[..., None, :]).reshape(T, PEER_HEADS, PEER_TOPK * PEER_TOPK)
    cidx = (i1[..., :, None] * PEER_NKEYS + i2[..., None, :]).reshape(T, PEER_HEADS, PEER_TOPK * PEER_TOPK)
    top_s, pos = lax.top_k(cand, PEER_TOPK)
    eidx = jnp.take_along_axis(cidx, pos, axis=-1)
    gate = jax.nn.softmax(top_s, axis=-1)
    nc = T // PEER_TOKEN_CHUNK
    xc = xt.reshape(nc, PEER_TOKEN_CHUNK, D)
    ec = eidx.reshape(nc, PEER_TOKEN_CHUNK, PEER_HEADS, PEER_TOPK)
    gc = gate.reshape(nc, PEER_TOKEN_CHUNK, PEER_HEADS, PEER_TOPK)

    def one(args):
        x_c, e_c, g_c = args
        u = expert_u[e_c]
        a = jnp.einsum('td,thkd->thk', x_c, u).astype(jnp.float32)
        w = (g_c * jax.nn.gelu(a, approximate=False)).astype(x_c.dtype)
        return jnp.einsum('thk,thkd->td', w, expert_v[e_c])

    out = lax.map(one, (xc, ec, gc))
    return out.reshape(B, S, D)


def setup_inputs(seed: int = 0) -> dict:
    key = jax.random.key(seed)
    ks = jax.random.split(key, 32)
    f32 = jnp.float32
    L = DEPTH

    def nrm(k, shape, scale):
        return jax.random.normal(k, shape, f32) * scale

    def gain(k, n):
        return 1.0 + 0.02 * jax.random.normal(k, (L, n), f32)

    offs = jax.random.randint(ks[2], (BATCH, 1), 0, 1024, dtype=jnp.int32)
    positions = offs + jnp.arange(SEQ, dtype=jnp.int32)[None, :]
    return {
        'x': nrm(ks[0], (BATCH, SEQ, D_MODEL), 1.0),
        'mem': nrm(ks[1], (BATCH, MEM_LEN, D_MODEL), 1.0),
        'positions': positions,
        'g_attn': gain(ks[3], D_MODEL),
        'w_in': nrm(ks[4], (L, D_MODEL, IN_COLS), D_MODEL ** -0.5),
        'g_q_a': gain(ks[5], MLA_Q_RANK),
        'w_uq': nrm(ks[6], (L, MLA_Q_RANK, MLA_HEADS * (MLA_NOPE + MLA_ROPE)), MLA_Q_RANK ** -0.5),
        'g_kv_a': gain(ks[7], MLA_KV_RANK),
        'w_ukv': nrm(ks[8], (L, MLA_KV_RANK, MLA_HEADS * (MLA_NOPE + MLA_V)), MLA_KV_RANK ** -0.5),
        'g_mla_o': gain(ks[9], MLA_OUT),
        'g_moba_o': gain(ks[10], MOBA_WIDTH),
        'w_out': nrm(ks[11], (L, MIX_WIDTH, D_MODEL), MIX_WIDTH ** -0.5),
        'g_cross': gain(ks[12], D_MODEL),
        'g_mem': gain(ks[13], D_MODEL),
        'w_cq': nrm(ks[14], (L, D_MODEL, D_MODEL), D_MODEL ** -0.5),
        'w_ck': nrm(ks[15], (L, D_MODEL, D_MODEL), D_MODEL ** -0.5),
        'w_cv': nrm(ks[16], (L, D_MODEL, D_MODEL), D_MODEL ** -0.5),
        'w_co': nrm(ks[17], (L, D_MODEL, D_MODEL), D_MODEL ** -0.5),
        'g_ffn': gain(ks[18], D_MODEL),
        'w_pq': nrm(ks[19], (L, D_MODEL, PEER_HEADS * PEER_DKEY), D_MODEL ** -0.5),
        'sub_keys1': nrm(ks[20], (L, PEER_NKEYS, PEER_DKEY // 2), (PEER_DKEY // 2) ** -0.5),
        'sub_keys2': nrm(ks[21], (L, PEER_NKEYS, PEER_DKEY // 2), (PEER_DKEY // 2) ** -0.5),
        'expert_u': nrm(ks[22], (L, PEER_N_EXPERTS, D_MODEL), D_MODEL ** -0.5),
        'expert_v': nrm(ks[23], (L, PEER_N_EXPERTS, D_MODEL), 0.5),
        'g_final': 1.0 + 0.02 * jax.random.normal(ks[24], (D_MODEL,), f32),
    }


def reference(x, mem, positions, g_attn, w_in, g_q_a, w_uq, g_kv_a, w_ukv, g_mla_o, g_moba_o,
              w_out, g_cross, g_mem, w_cq, w_ck, w_cv, w_co, g_ffn, w_pq, sub_keys1, sub_keys2,
              expert_u, expert_v, g_final):
    B, S, D = x.shape
    h = x
    for l in range(DEPTH):
        xn = rms_norm(h, g_attn[l])
        proj = xn @ w_in[l]
        c_q, c_kv, k_r, q_m, k_m, v_m = jnp.split(proj, IN_SPLITS, axis=-1)
        q = (rms_norm(c_q, g_q_a[l]) @ w_uq[l]).reshape(B, S, MLA_HEADS, MLA_NOPE + MLA_ROPE)
        q = jnp.concatenate([q[..., :MLA_NOPE], rope(q[..., MLA_NOPE:], positions)], axis=-1)
        kv = (rms_norm(c_kv, g_kv_a[l]) @ w_ukv[l]).reshape(B, S, MLA_HEADS, MLA_NOPE + MLA_V)
        k_nope, v = kv[..., :MLA_NOPE], kv[..., MLA_NOPE:]
        k_rope = rope(k_r.reshape(B, S, 1, MLA_ROPE), positions)
        k = jnp.concatenate([k_nope, jnp.broadcast_to(k_rope, (B, S, MLA_HEADS, MLA_ROPE))], axis=-1)
        o_mla = causal_block_attention(q.transpose(0, 2, 1, 3), k.transpose(0, 2, 1, 3),
                                       v.transpose(0, 2, 1, 3), (MLA_NOPE + MLA_ROPE) ** -0.5)
        qm = rope(q_m.reshape(B, S, MOBA_HEADS, MOBA_HEAD_DIM), positions).transpose(0, 2, 1, 3)
        km = rope(k_m.reshape(B, S, MOBA_HEADS, MOBA_HEAD_DIM), positions).transpose(0, 2, 1, 3)
        vm = v_m.reshape(B, S, MOBA_HEADS, MOBA_HEAD_DIM).transpose(0, 2, 1, 3)
        o_moba = moba_attention(qm, km, vm)
        mixed = jnp.concatenate([rms_norm(o_mla, g_mla_o[l]), rms_norm(o_moba, g_moba_o[l])], axis=-1)
        h = h + mixed @ w_out[l]
        h = h + memory_cross_attention(rms_norm(h, g_cross[l]), rms_norm(mem, g_mem[l]),
                                       w_cq[l], w_ck[l], w_cv[l], w_co[l])
        h = h + peer(rms_norm(h, g_ffn[l]), w_pq[l], sub_keys1[l], sub_keys2[l], expert_u[l], expert_v[l])
    return rms_norm(h, g_final)
```

```python
import functools
import math

import jax
import jax.numpy as jnp
from jax import lax
from jax.experimental import pallas as pl
from jax.experimental.pallas import tpu as pltpu

F32 = jnp.float32
BF16 = jnp.bfloat16

EPS = 1e-6
ROPE_THETA = 10000.0
MLA_HEADS = 8
MLA_Q_RANK = 512
MLA_KV_RANK = 256
MLA_NOPE = 128
MLA_ROPE = 64
MLA_V = 128
MLA_QK_PAD = 256
MOBA_HEADS = 8
MOBA_HEAD_DIM = 128
MOBA_BLOCK = 256
MOBA_TOPK = 3
CROSS_HEADS = 4
PEER_HEADS = 8
PEER_NKEYS = 128
PEER_DKEY = 256
PEER_TOPK = 16

LANES = 128
VMEM_LIMIT = 56 * 1024 * 1024

NEG_INF = float("-inf")
NT_DIMS = (((1,), (1,)), ((), ()))


def _rms(x, g):
    ms = jnp.mean(x * x, axis=-1, keepdims=True)
    return x * lax.rsqrt(ms + EPS) * g


def _dot(a, b):
    return jnp.dot(a, b, preferred_element_type=F32)


def _dot_nt(a, b, precision=None):
    return lax.dot_general(a, b, NT_DIMS, precision=precision, preferred_element_type=F32)


def _const_spec(shape):
    nd = len(shape)
    return pl.BlockSpec(shape, lambda *_: (0,) * nd, pipeline_mode=pl.Buffered(1))


def _params(*sem):
    return pltpu.CompilerParams(dimension_semantics=sem, vmem_limit_bytes=VMEM_LIMIT)


def _rope_table_kernel(pos_ref, cm_ref, sm_ref, cl_ref, sl_ref):
    pos = pos_ref[...].astype(F32)
    lane = lax.broadcasted_iota(jnp.int32, (1, LANES), 1)
    half = MOBA_HEAD_DIM // 2
    inv = jnp.exp((lane % half).astype(F32) * (-2.0 * math.log(ROPE_THETA) / MOBA_HEAD_DIM))
    ang = pos * inv
    cm_ref[...] = jnp.cos(ang)
    sn = jnp.sin(ang)
    sm_ref[...] = jnp.where(lane < half, -sn, sn)
    half = MLA_ROPE // 2
    inv = jnp.exp((lane % half).astype(F32) * (-2.0 * math.log(ROPE_THETA) / MLA_ROPE))
    ang = pos * inv
    valid = lane < MLA_ROPE
    cl_ref[...] = jnp.where(valid, jnp.cos(ang), 0.0)
    sn = jnp.sin(ang)
    sl_ref[...] = jnp.where(lane < half, -sn, jnp.where(valid, sn, 0.0))


def _rope_tables(pos_col):
    T = pos_col.shape[0]
    tm = 512
    out = jax.ShapeDtypeStruct((T, LANES), F32)
    spec = pl.BlockSpec((tm, LANES), lambda i: (i, 0))
    return pl.pallas_call(
        _rope_table_kernel,
        grid=(T // tm,),
        in_specs=[pl.BlockSpec((tm, 1), lambda i: (i, 0))],
        out_specs=[spec] * 4,
        out_shape=[out] * 4,
        compiler_params=_params("parallel"),
        name="rope_tables",
    )(pos_col)


def _rope_moba(x, cos, sin):
    return x * cos + pltpu.roll(x, MOBA_HEAD_DIM // 2, 1) * sin


def _rope_mla(x, cos, sin, lane):
    half = MLA_ROPE // 2
    partner = jnp.where(lane < half, pltpu.roll(x, LANES - half, 1), pltpu.roll(x, half, 1))
    return x * cos + partner * sin


C_COLS = MLA_Q_RANK + MLA_KV_RANK + LANES
MOBA_W = MOBA_HEADS * MOBA_HEAD_DIM


def _attn_prep_kernel(x_ref, g_ref, w_in_ref, gq_ref, wuq_ref, gkv_ref, wuk_ref, wuv_ref,
                      cm_ref, sm_ref, cl_ref, sl_ref,
                      ql_ref, kl_ref, vl_ref, qm_ref, km_ref, vm_ref, sel_ref,
                      kbar_ref, qmf_ref, kmf_ref, *, tm, tiles_per_seq, n_blocks):
    i = pl.program_id(0)

    @pl.when(i == 0)
    def _():
        kbar_ref[...] = jnp.zeros_like(kbar_ref)

    xn = _rms(x_ref[...], g_ref[...]).astype(BF16)
    lane = lax.broadcasted_iota(jnp.int32, (1, LANES), 1)
    cl, sl = cl_ref[...], sl_ref[...]
    cm, sm = cm_ref[...], sm_ref[...]

    c = _dot(xn, w_in_ref[:, 0:C_COLS])
    cq = _rms(c[:, 0:MLA_Q_RANK], gq_ref[...]).astype(BF16)
    q = _dot(cq, wuq_ref[...])
    ckv = _rms(c[:, MLA_Q_RANK:MLA_Q_RANK + MLA_KV_RANK], gkv_ref[...]).astype(BF16)
    kn = _dot(ckv, wuk_ref[...])
    vl_ref[...] = _dot(ckv, wuv_ref[...]).astype(BF16)
    kr = _rope_mla(c[:, MLA_Q_RANK + MLA_KV_RANK:C_COLS], cl, sl, lane).astype(BF16)
    for h in range(MLA_HEADS):
        o = h * MLA_QK_PAD
        ql_ref[:, o:o + LANES] = q[:, o:o + LANES].astype(BF16)
        ql_ref[:, o + LANES:o + 2 * LANES] = _rope_mla(q[:, o + LANES:o + 2 * LANES], cl, sl, lane).astype(BF16)
        kl_ref[:, o:o + LANES] = kn[:, h * LANES:(h + 1) * LANES].astype(BF16)
        kl_ref[:, o + LANES:o + 2 * LANES] = kr

    o0 = C_COLS
    qm = _dot(xn, w_in_ref[:, o0:o0 + MOBA_W])
    km = _dot(xn, w_in_ref[:, o0 + MOBA_W:o0 + 2 * MOBA_W])
    vm_ref[...] = _dot(xn, w_in_ref[:, o0 + 2 * MOBA_W:o0 + 3 * MOBA_W]).astype(BF16)
    for h in range(MOBA_HEADS):
        sl_h = slice(h * LANES, (h + 1) * LANES)
        qr = _rope_moba(qm[:, sl_h], cm, sm)
        kr_h = _rope_moba(km[:, sl_h], cm, sm)
        qmf_ref[:, sl_h] = qr
        kmf_ref[:, sl_h] = kr_h
        qm_ref[:, sl_h] = qr.astype(BF16)
        km_ref[:, sl_h] = kr_h.astype(BF16)

    blk0 = (i % tiles_per_seq) * (tm // MOBA_BLOCK)
    for s in range(tm // MOBA_BLOCK):
        kbar_ref[pl.ds(blk0 + s, 1), :] = jnp.mean(
            kmf_ref[s * MOBA_BLOCK:(s + 1) * MOBA_BLOCK, :], axis=0, keepdims=True)
    tok = lax.broadcasted_iota(jnp.int32, (n_blocks, tm), 1)
    own = blk0 + tok // MOBA_BLOCK
    blk = lax.broadcasted_iota(jnp.int32, (n_blocks, tm), 0)
    past = blk < own
    rows = []
    for h in range(MOBA_HEADS):
        sl_h = slice(h * LANES, (h + 1) * LANES)
        gate = _dot_nt(kbar_ref[:, sl_h], qmf_ref[:, sl_h], precision=lax.Precision.HIGHEST)
        gate = jnp.where(past, gate, NEG_INF)
        ahead = jnp.zeros((n_blocks, tm), jnp.int32)
        for m in range(n_blocks):
            gm = gate[m:m + 1, :]
            beats = (gm > gate) | ((gm == gate) & (m < blk))
            ahead = ahead + beats.astype(jnp.int32)
        rows.append(jnp.where(past & (ahead < MOBA_TOPK), 1.0, 0.0).astype(F32))
    pad = jnp.zeros((LANES - MOBA_HEADS * n_blocks, tm), F32)
    sel_ref[...] = jnp.concatenate(rows + [pad], axis=0).T


def _attn_prep(x2, g_attn, w_in_p, g_q, w_uq_p, g_kv, w_uk, w_uv, tabs, *, seq):
    T, D = x2.shape
    tm = 512
    n_blocks = seq // MOBA_BLOCK
    assert seq % tm == 0 and tm % MOBA_BLOCK == 0 and MOBA_HEADS * n_blocks <= LANES
    cm, sm, cl, sl = tabs
    row = lambda w: pl.BlockSpec((tm, w), lambda i: (i, 0))
    kern = functools.partial(_attn_prep_kernel, tm=tm, tiles_per_seq=seq // tm, n_blocks=n_blocks)
    outs = [(MLA_HEADS * MLA_QK_PAD, BF16), (MLA_HEADS * MLA_QK_PAD, BF16), (MLA_HEADS * MLA_V, BF16),
            (MOBA_W, BF16), (MOBA_W, BF16), (MOBA_W, BF16), (LANES, F32)]
    return pl.pallas_call(
        kern,
        grid=(T // tm,),
        in_specs=[row(D), _const_spec(g_attn.shape), _const_spec(w_in_p.shape), _const_spec(g_q.shape),
                  _const_spec(w_uq_p.shape), _const_spec(g_kv.shape), _const_spec(w_uk.shape),
                  _const_spec(w_uv.shape), row(LANES), row(LANES), row(LANES), row(LANES)],
        out_specs=[row(w) for w, _ in outs],
        out_shape=[jax.ShapeDtypeStruct((T, w), dt) for w, dt in outs],
        scratch_shapes=[pltpu.VMEM((n_blocks, MOBA_W), F32), pltpu.VMEM((tm, MOBA_W), F32),
                        pltpu.VMEM((tm, MOBA_W), F32)],
        compiler_params=_params("arbitrary"),
        name="attn_prep",
    )(x2, g_attn, w_in_p, g_q, w_uq_p, g_kv, w_uk, w_uv, cm, sm, cl, sl)


ATTN_TQ = 256


def _softmax_av(s, v):
    m = jnp.max(s, axis=-1, keepdims=True)
    e = jnp.exp(s - m)
    l = jnp.sum(e, axis=-1, keepdims=True)
    return _dot(e.astype(BF16), v) / l


def _causal_bias(tq):
    r = lax.broadcasted_iota(jnp.int32, (tq, tq), 0)
    c = lax.broadcasted_iota(jnp.int32, (tq, tq), 1)
    return jnp.where(c <= r, 0.0, NEG_INF).astype(F32)


def _mla_attn_kernel(q_ref, k_ref, v_ref, o_ref, *, seq, scale):
    tq = ATTN_TQ
    diag = _causal_bias(tq)
    for i in range(seq // tq):
        n = (i + 1) * tq
        s = _dot_nt(q_ref[i * tq:n, :], k_ref[0:n, :]) * scale
        if i:
            bias = jnp.concatenate([jnp.zeros((tq, i * tq), F32), diag], axis=1)
        else:
            bias = diag
        o_ref[i * tq:n, :] = _softmax_av(s + bias, v_ref[0:n, :])


def _mla_attention(ql, kl, vl, *, batch, seq):
    T = ql.shape[0]
    scale = (MLA_NOPE + MLA_ROPE) ** -0.5
    return pl.pallas_call(
        functools.partial(_mla_attn_kernel, seq=seq, scale=scale),
        grid=(batch, MLA_HEADS),
        in_specs=[pl.BlockSpec((seq, MLA_QK_PAD), lambda b, h: (b, h)),
                  pl.BlockSpec((seq, MLA_QK_PAD), lambda b, h: (b, h)),
                  pl.BlockSpec((seq, MLA_V), lambda b, h: (b, h))],
        out_specs=pl.BlockSpec((seq, MLA_V), lambda b, h: (b, h)),
        out_shape=jax.ShapeDtypeStruct((T, MLA_HEADS * MLA_V), F32),
        compiler_params=_params("parallel", "parallel"),
        name="mla_attention",
    )(ql, kl, vl)


def _moba_attn_kernel(q_ref, k_ref, v_ref, sel_ref, o_ref, *, seq, scale, n_blocks):
    tq = ATTN_TQ
    h = pl.program_id(1)
    lane = lax.broadcasted_iota(jnp.int32, (1, LANES), 1)
    diag = _causal_bias(tq)
    for i in range(seq // tq):
        n = (i + 1) * tq
        s = _dot_nt(q_ref[i * tq:n, :], k_ref[0:n, :]) * scale
        sel = sel_ref[i * tq:n, :]
        pieces = []
        for b in range(i):
            chosen = jnp.sum(jnp.where(lane == h * n_blocks + b, sel, 0.0), axis=-1, keepdims=True)
            pieces.append(jnp.broadcast_to(jnp.where(chosen > 0.0, 0.0, NEG_INF), (tq, tq)))
        bias = jnp.concatenate(pieces + [diag], axis=1) if pieces else diag
        o_ref[i * tq:n, :] = _softmax_av(s + bias, v_ref[0:n, :])


def _moba_attention(qm, km, vm, sel, *, batch, seq):
    T = qm.shape[0]
    assert ATTN_TQ == MOBA_BLOCK
    hd = pl.BlockSpec((seq, MOBA_HEAD_DIM), lambda b, h: (b, h))
    return pl.pallas_call(
        functools.partial(_moba_attn_kernel, seq=seq, scale=MOBA_HEAD_DIM ** -0.5,
                          n_blocks=seq // MOBA_BLOCK),
        grid=(batch, MOBA_HEADS),
        in_specs=[hd, hd, hd, pl.BlockSpec((seq, LANES), lambda b, h: (b, 0))],
        out_specs=hd,
        out_shape=jax.ShapeDtypeStruct((T, MOBA_W), F32),
        compiler_params=_params("parallel", "parallel"),
        name="moba_attention",
    )(qm, km, vm, sel)


def _out_proj_kernel(x_ref, ol_ref, om_ref, gl_ref, gm_ref, w_ref, h_ref, *, wl):
    a = _rms(ol_ref[...], gl_ref[...]).astype(BF16)
    b = _rms(om_ref[...], gm_ref[...]).astype(BF16)
    h_ref[...] = x_ref[...] + (_dot(a, w_ref[0:wl, :]) + _dot(b, w_ref[wl:, :]))


def _out_proj(x2, o_mla, o_moba, g_l, g_m, w_out):
    T, D = x2.shape
    tm = 512
    row = lambda w: pl.BlockSpec((tm, w), lambda i: (i, 0))
    return pl.pallas_call(
        functools.partial(_out_proj_kernel, wl=o_mla.shape[1]),
        grid=(T // tm,),
        in_specs=[row(D), row(o_mla.shape[1]), row(o_moba.shape[1]), _const_spec(g_l.shape),
                  _const_spec(g_m.shape), _const_spec(w_out.shape)],
        out_specs=row(D),
        out_shape=jax.ShapeDtypeStruct((T, D), F32),
        compiler_params=_params("parallel"),
        name="out_proj",
    )(x2, o_mla, o_moba, g_l, g_m, w_out)


def _rms_matmul_kernel(x_ref, g_ref, w_ref, o_ref):
    xn = _rms(x_ref[...], g_ref[...]).astype(BF16)
    o_ref[...] = _dot(xn, w_ref[...]).astype(o_ref.dtype)


def _rms_matmul(x, g, w, out_dtype, *, tm, tn):
    M, K = x.shape
    N = w.shape[1]
    return pl.pallas_call(
        _rms_matmul_kernel,
        grid=(M // tm, N // tn),
        in_specs=[pl.BlockSpec((tm, K), lambda i, j: (i, 0)), pl.BlockSpec((1, K), lambda i, j: (0, 0)),
                  pl.BlockSpec((K, tn), lambda i, j: (0, j))],
        out_specs=pl.BlockSpec((tm, tn), lambda i, j: (i, j)),
        out_shape=jax.ShapeDtypeStruct((M, N), out_dtype),
        compiler_params=_params("parallel", "parallel"),
        name="rms_matmul",
    )(x, g, w)


def _cross_kernel(h_ref, g_ref, wq_ref, kv_ref, wo_ref, o_ref, *, scale):
    h = h_ref[...]
    D = h.shape[1]
    dh = D // CROSS_HEADS
    q = _dot(_rms(h, g_ref[...]).astype(BF16), wq_ref[...]).astype(BF16)
    outs = []
    for hd in range(CROSS_HEADS):
        s = _dot_nt(q[:, hd * dh:(hd + 1) * dh], kv_ref[:, hd * dh:(hd + 1) * dh]) * scale
        outs.append(_softmax_av(s, kv_ref[:, D + hd * dh:D + (hd + 1) * dh]).astype(BF16))
    o_ref[...] = h + _dot(jnp.concatenate(outs, axis=1), wo_ref[...])


def _cross_attention(h1, g_cross, w_cq, kv, w_co, *, seq, mem_len):
    T, D = h1.shape
    tm = 256
    tiles_per_seq = seq // tm
    row = pl.BlockSpec((tm, D), lambda i: (i, 0))
    return pl.pallas_call(
        functools.partial(_cross_kernel, scale=(D // CROSS_HEADS) ** -0.5),
        grid=(T // tm,),
        in_specs=[row, _const_spec(g_cross.shape), _const_spec(w_cq.shape),
                  pl.BlockSpec((mem_len, 2 * D), lambda i: (i // tiles_per_seq, 0)),
                  _const_spec(w_co.shape)],
        out_specs=row,
        out_shape=jax.ShapeDtypeStruct((T, D), F32),
        compiler_params=_params("parallel"),
        name="cross_attention",
    )(h1, g_cross, w_cq, kv, w_co)


def _top16_cols(s):
    n, tm = s.shape
    idx = lax.broadcasted_iota(jnp.int32, (n, tm), 0)
    rank = jnp.full((n, tm), PEER_TOPK, jnp.int32)
    vals = []
    for r in range(PEER_TOPK):
        m = jnp.max(s, axis=0, keepdims=True)
        first = jnp.min(jnp.where(s == m, idx, n), axis=0, keepdims=True)
        hit = idx == first
        rank = jnp.where(hit, r, rank)
        s = jnp.where(hit, NEG_INF, s)
        vals.append(m)
    return vals, rank


_PAIR_B = [PEER_TOPK // (a + 1) for a in range(8)]


def _pair_counts(v1, v2):
    tm = v1[0].shape[1]
    v1s = jnp.concatenate(v1, axis=0)
    v2s = jnp.concatenate(v2, axis=0)
    pieces, keys = [], []
    for a, nb in enumerate(_PAIR_B):
        rows = -(-nb // 8) * 8
        r = lax.broadcasted_iota(jnp.int32, (rows, tm), 0)
        pieces.append(jnp.where(r < nb, v2s[0:rows, :] + v1[a], NEG_INF))
        keys.append(a * PEER_TOPK + r)
    r = lax.broadcasted_iota(jnp.int32, (8, tm), 0)
    pieces.append(v1s[8:16, :] + v2[0])
    keys.append((r + 8) * PEER_TOPK)
    cand = jnp.concatenate(pieces, axis=0)
    key = jnp.concatenate(keys, axis=0)
    big = PEER_TOPK * PEER_TOPK
    work = cand
    chosen = jnp.zeros(cand.shape, jnp.bool_)
    for _ in range(PEER_TOPK):
        m = jnp.max(work, axis=0, keepdims=True)
        first = jnp.min(jnp.where(work == m, key, big), axis=0, keepdims=True)
        hit = key == first
        chosen = chosen | hit
        work = jnp.where(hit, NEG_INF, work)
    top = v1[0] + v2[0]
    z = jnp.sum(jnp.where(chosen, jnp.exp(cand - top), 0.0), axis=0, keepdims=True)
    picked = chosen.astype(F32)
    counts, o = [], 0
    for nb in _PAIR_B:
        rows = -(-nb // 8) * 8
        counts.append(jnp.sum(picked[o:o + rows, :], axis=0, keepdims=True))
        o += rows
    for a in range(8):
        counts.append(picked[o + a:o + a + 1, :])
    return counts, z


def _peer_select_kernel(h_ref, g_ref, wq_ref, k1_ref, k2_ref, xn_ref, e1_ref, cnt_ref, rk2_ref, e2_ref):
    xn = _rms(h_ref[...], g_ref[...]).astype(BF16)
    xn_ref[...] = xn
    q = _dot(xn, wq_ref[...])
    half = PEER_DKEY // 2
    hi = lax.Precision.HIGHEST
    for hd in range(PEER_HEADS):
        o = hd * PEER_DKEY
        s1 = _dot_nt(k1_ref[...], q[:, o:o + half], precision=hi)
        s2 = _dot_nt(k2_ref[...], q[:, o + half:o + 2 * half], precision=hi)
        v1, rank1 = _top16_cols(s1)
        v2, rank2 = _top16_cols(s2)
        counts, z = _pair_counts(v1, v2)
        cnt = jnp.zeros(s1.shape, F32)
        for a in range(PEER_TOPK):
            cnt = jnp.where(rank1 == a, counts[a], cnt)
        e1_ref[hd] = jnp.exp(s1 - v1[0])
        cnt_ref[hd] = cnt
        rk2_ref[hd] = rank2.astype(F32)
        e2_ref[hd] = jnp.exp(s2 - v2[0]) / z


def _peer_select(h2, g_ffn, w_pq, k1, k2):
    T, D = h2.shape
    tm = 256
    row = pl.BlockSpec((tm, D), lambda i: (i, 0))
    hk = pl.BlockSpec((PEER_HEADS, PEER_NKEYS, tm), lambda i: (0, 0, i))
    hk_shape = jax.ShapeDtypeStruct((PEER_HEADS, PEER_NKEYS, T), F32)
    return pl.pallas_call(
        _peer_select_kernel,
        grid=(T // tm,),
        in_specs=[row, _const_spec(g_ffn.shape), _const_spec(w_pq.shape), _const_spec(k1.shape),
                  _const_spec(k2.shape)],
        out_specs=[row, hk, hk, hk, hk],
        out_shape=[jax.ShapeDtypeStruct((T, D), BF16), hk_shape, hk_shape, hk_shape, hk_shape],
        compiler_params=_params("parallel"),
        name="peer_select",
    )(h2, g_ffn, w_pq, k1, k2)


def _peer_dense_kernel(xn_ref, u_ref, vt_ref, e1_ref, cnt_ref, rk2_ref, e2_ref, h_ref, gf_ref,
                       o_ref, acc_ref, *, te):
    j = pl.program_id(1)

    @pl.when(j == 0)
    def _():
        acc_ref[...] = jnp.zeros_like(acc_ref)

    a_t = _dot_nt(u_ref[...], xn_ref[...])
    act = 0.5 * a_t * (1.0 + lax.erf(a_t * math.sqrt(0.5)))
    rows_per_step = te // PEER_NKEYS
    pieces = []
    for r in range(rows_per_step):
        i1 = j * rows_per_step + r
        g = None
        for hd in range(PEER_HEADS):
            e1 = e1_ref[hd, pl.ds(i1, 1), :]
            cnt = cnt_ref[hd, pl.ds(i1, 1), :]
            term = e1 * jnp.where(rk2_ref[hd] < cnt, e2_ref[hd], 0.0)
            g = term if g is None else g + term
        pieces.append((g * act[r * PEER_NKEYS:(r + 1) * PEER_NKEYS, :]).astype(BF16))
    w_t = jnp.concatenate(pieces, axis=0) if len(pieces) > 1 else pieces[0]
    acc_ref[...] += _dot(vt_ref[...], w_t)

    @pl.when(j == pl.num_programs(1) - 1)
    def _():
        o_ref[...] = _rms(h_ref[...] + acc_ref[...].T, gf_ref[...])


def _peer_dense(xn, u, vt, e1, cnt, rk2, e2, h2, g_final):
    T, D = h2.shape
    E = u.shape[0]
    tm, te = 512, 512
    row = pl.BlockSpec((tm, D), lambda i, j: (i, 0))
    hk = pl.BlockSpec((PEER_HEADS, PEER_NKEYS, tm), lambda i, j: (0, 0, i))
    return pl.pallas_call(
        functools.partial(_peer_dense_kernel, te=te),
        grid=(T // tm, E // te),
        in_specs=[row, pl.BlockSpec((te, D), lambda i, j: (j, 0)), pl.BlockSpec((D, te), lambda i, j: (0, j)),
                  hk, hk, hk, hk, row, pl.BlockSpec((1, D), lambda i, j: (0, 0))],
        out_specs=row,
        out_shape=jax.ShapeDtypeStruct((T, D), F32),
        scratch_shapes=[pltpu.VMEM((D, tm), F32)],
        compiler_params=_params("parallel", "arbitrary"),
        name="peer_dense",
    )(xn, u, vt, e1, cnt, rk2, e2, h2, g_final)


def _layer(h, mem2, tabs, g_attn, w_in, g_q_a, w_uq, g_kv_a, w_ukv, g_mla_o, g_moba_o, w_out, g_cross,
           g_mem, w_cq, w_ck, w_cv, w_co, g_ffn, w_pq, sub_keys1, sub_keys2, expert_u, expert_v,
           g_final, *, batch, seq, mem_len):
    D = h.shape[1]
    row = lambda g: g.reshape(1, -1)
    n_lat = MLA_Q_RANK + MLA_KV_RANK + MLA_ROPE
    w_in_p = jnp.concatenate([w_in[:, :n_lat], jnp.zeros((D, LANES - MLA_ROPE), w_in.dtype),
                              w_in[:, n_lat:]], axis=1).astype(BF16)
    w_uq_p = jnp.pad(w_uq.reshape(MLA_Q_RANK, MLA_HEADS, MLA_NOPE + MLA_ROPE),
                     ((0, 0), (0, 0), (0, MLA_QK_PAD - MLA_NOPE - MLA_ROPE))
                     ).reshape(MLA_Q_RANK, MLA_HEADS * MLA_QK_PAD).astype(BF16)
    w_ukv3 = w_ukv.reshape(MLA_KV_RANK, MLA_HEADS, MLA_NOPE + MLA_V)
    w_uk = w_ukv3[:, :, :MLA_NOPE].reshape(MLA_KV_RANK, MLA_HEADS * MLA_NOPE).astype(BF16)
    w_uv = w_ukv3[:, :, MLA_NOPE:].reshape(MLA_KV_RANK, MLA_HEADS * MLA_V).astype(BF16)

    ql, kl, vl, qm, km, vm, sel = _attn_prep(h, row(g_attn), w_in_p, row(g_q_a), w_uq_p, row(g_kv_a),
                                             w_uk, w_uv, tabs, seq=seq)
    o_mla = _mla_attention(ql, kl, vl, batch=batch, seq=seq)
    o_moba = _moba_attention(qm, km, vm, sel, batch=batch, seq=seq)
    h1 = _out_proj(h, o_mla, o_moba, row(g_mla_o), row(g_moba_o), w_out.astype(BF16))

    w_ckv = jnp.concatenate([w_ck, w_cv], axis=1).astype(BF16)
    kv = _rms_matmul(mem2, row(g_mem), w_ckv, BF16, tm=min(mem2.shape[0], 512), tn=1024)
    h2 = _cross_attention(h1, row(g_cross), w_cq.astype(BF16), kv, w_co.astype(BF16), seq=seq, mem_len=mem_len)

    xn, e1, cnt, rk2, e2 = _peer_select(h2, row(g_ffn), w_pq.astype(BF16), sub_keys1, sub_keys2)
    return _peer_dense(xn, expert_u.astype(BF16), expert_v.astype(BF16).T, e1, cnt, rk2, e2, h2, row(g_final))


def kernel(x, mem, positions, g_attn, w_in, g_q_a, w_uq, g_kv_a, w_ukv, g_mla_o, g_moba_o, w_out, g_cross,
           g_mem, w_cq, w_ck, w_cv, w_co, g_ffn, w_pq, sub_keys1, sub_keys2, expert_u, expert_v, g_final):
    B, S, D = x.shape
    M = mem.shape[1]
    depth = w_in.shape[0]
    assert depth == 1, "the final norm is fused into the layer's last kernel"
    tabs = _rope_tables(positions.reshape(B * S, 1))
    out = _layer(x.reshape(B * S, D), mem.reshape(B * M, D), tabs, g_attn[0], w_in[0], g_q_a[0], w_uq[0],
                 g_kv_a[0], w_ukv[0], g_mla_o[0], g_moba_o[0], w_out[0], g_cross[0], g_mem[0], w_cq[0],
                 w_ck[0], w_cv[0], w_co[0], g_ffn[0], w_pq[0], sub_keys1[0], sub_keys2[0], expert_u[0],
                 expert_v[0], g_final, batch=B, seq=S, mem_len=M)
    return out.reshape(B, S, D)
```

```python
import functools
import math

import jax
import jax.numpy as jnp
from jax import lax
from jax.experimental import pallas as pl
from jax.experimental.pallas import tpu as pltpu

F32 = jnp.float32
BF16 = jnp.bfloat16

EPS = 1e-6
ROPE_THETA = 10000.0
MLA_HEADS = 8
MLA_Q_RANK = 512
MLA_KV_RANK = 256
MLA_NOPE = 128
MLA_ROPE = 64
MLA_V = 128
MLA_QK_PAD = 256
MOBA_HEADS = 8
MOBA_HEAD_DIM = 128
MOBA_BLOCK = 256
MOBA_TOPK = 3
CROSS_HEADS = 4
PEER_HEADS = 8
PEER_NKEYS = 128
PEER_DKEY = 256
PEER_TOPK = 16

LANES = 128
VMEM_LIMIT = 56 * 1024 * 1024

NEG_INF = float("-inf")
NT_DIMS = (((1,), (1,)), ((), ()))


def _rms(x, g):
    ms = jnp.mean(x * x, axis=-1, keepdims=True)
    return x * lax.rsqrt(ms + EPS) * g


def _dot(a, b):
    return jnp.dot(a, b, preferred_element_type=F32)


def _dot_nt(a, b, precision=None):
    return lax.dot_general(a, b, NT_DIMS, precision=precision, preferred_element_type=F32)


def _const_spec(shape):
    nd = len(shape)
    return pl.BlockSpec(shape, lambda *_: (0,) * nd, pipeline_mode=pl.Buffered(1))


def _params(*sem):
    return pltpu.CompilerParams(dimension_semantics=sem, vmem_limit_bytes=VMEM_LIMIT)


def _rope_table_kernel(pos_ref, cm_ref, sm_ref, cl_ref, sl_ref):
    pos = pos_ref[...].astype(F32)
    lane = lax.broadcasted_iota(jnp.int32, (1, LANES), 1)
    half = MOBA_HEAD_DIM // 2
    inv = jnp.exp((lane % half).astype(F32) * (-2.0 * math.log(ROPE_THETA) / MOBA_HEAD_DIM))
    ang = pos * inv
    cm_ref[...] = jnp.cos(ang)
    sn = jnp.sin(ang)
    sm_ref[...] = jnp.where(lane < half, -sn, sn)
    half = MLA_ROPE // 2
    inv = jnp.exp((lane % half).astype(F32) * (-2.0 * math.log(ROPE_THETA) / MLA_ROPE))
    ang = pos * inv
    valid = lane < MLA_ROPE
    cl_ref[...] = jnp.where(valid, jnp.cos(ang), 0.0)
    sn = jnp.sin(ang)
    sl_ref[...] = jnp.where(lane < half, -sn, jnp.where(valid, sn, 0.0))


def _rope_tables(pos_col):
    T = pos_col.shape[0]
    tm = 512
    out = jax.ShapeDtypeStruct((T, LANES), F32)
    spec = pl.BlockSpec((tm, LANES), lambda i: (i, 0))
    return pl.pallas_call(
        _rope_table_kernel,
        grid=(T // tm,),
        in_specs=[pl.BlockSpec((tm, 1), lambda i: (i, 0))],
        out_specs=[spec] * 4,
        out_shape=[out] * 4,
        compiler_params=_params("parallel"),
        name="rope_tables",
    )(pos_col)


def _rope_moba(x, cos, sin):
    return x * cos + pltpu.roll(x, MOBA_HEAD_DIM // 2, 1) * sin


def _rope_mla(x, cos, sin, lane):
    half = MLA_ROPE // 2
    partner = jnp.where(lane < half, pltpu.roll(x, LANES - half, 1), pltpu.roll(x, half, 1))
    return x * cos + partner * sin


C_COLS = MLA_Q_RANK + MLA_KV_RANK + LANES
MOBA_W = MOBA_HEADS * MOBA_HEAD_DIM


def _attn_prep_kernel(x_ref, g_ref, w_in_ref, gq_ref, wuq_ref, gkv_ref, wuk_ref, wuv_ref,
                      cm_ref, sm_ref, cl_ref, sl_ref,
                      ql_ref, kl_ref, vl_ref, qm_ref, km_ref, vm_ref, sel_ref,
                      kbar_ref, qmf_ref, kmf_ref, *, tm, tiles_per_seq, n_blocks):
    i = pl.program_id(0)

    @pl.when(i == 0)
    def _():
        kbar_ref[...] = jnp.zeros_like(kbar_ref)

    xn = _rms(x_ref[...], g_ref[...]).astype(BF16)
    lane = lax.broadcasted_iota(jnp.int32, (1, LANES), 1)
    cl, sl = cl_ref[...], sl_ref[...]
    cm, sm = cm_ref[...], sm_ref[...]

    c = _dot(xn, w_in_ref[:, 0:C_COLS])
    cq = _rms(c[:, 0:MLA_Q_RANK], gq_ref[...]).astype(BF16)
    q = _dot(cq, wuq_ref[...])
    ckv = _rms(c[:, MLA_Q_RANK:MLA_Q_RANK + MLA_KV_RANK], gkv_ref[...]).astype(BF16)
    kn = _dot(ckv, wuk_ref[...])
    vl_ref[...] = _dot(ckv, wuv_ref[...]).astype(BF16)
    kr = _rope_mla(c[:, MLA_Q_RANK + MLA_KV_RANK:C_COLS], cl, sl, lane).astype(BF16)
    for h in range(MLA_HEADS):
        o = h * MLA_QK_PAD
        ql_ref[:, o:o + LANES] = q[:, o:o + LANES].astype(BF16)
        ql_ref[:, o + LANES:o + 2 * LANES] = _rope_mla(q[:, o + LANES:o + 2 * LANES], cl, sl, lane).astype(BF16)
        kl_ref[:, o:o + LANES] = kn[:, h * LANES:(h + 1) * LANES].astype(BF16)
        kl_ref[:, o + LANES:o + 2 * LANES] = kr

    o0 = C_COLS
    qm = _dot(xn, w_in_ref[:, o0:o0 + MOBA_W])
    km = _dot(xn, w_in_ref[:, o0 + MOBA_W:o0 + 2 * MOBA_W])
    vm_ref[...] = _dot(xn, w_in_ref[:, o0 + 2 * MOBA_W:o0 + 3 * MOBA_W]).astype(BF16)
    for h in range(MOBA_HEADS):
        sl_h = slice(h * LANES, (h + 1) * LANES)
        qr = _rope_moba(qm[:, sl_h], cm, sm)
        kr_h = _rope_moba(km[:, sl_h], cm, sm)
        qmf_ref[:, sl_h] = qr
        kmf_ref[:, sl_h] = kr_h
        qm_ref[:, sl_h] = qr.astype(BF16)
        km_ref[:, sl_h] = kr_h.astype(BF16)

    blk0 = (i % tiles_per_seq) * (tm // MOBA_BLOCK)
    for s in range(tm // MOBA_BLOCK):
        kbar_ref[pl.ds(blk0 + s, 1), :] = jnp.mean(
            kmf_ref[s * MOBA_BLOCK:(s + 1) * MOBA_BLOCK, :], axis=0, keepdims=True)
    tok = lax.broadcasted_iota(jnp.int32, (n_blocks, tm), 1)
    own = blk0 + tok // MOBA_BLOCK
    blk = lax.broadcasted_iota(jnp.int32, (n_blocks, tm), 0)
    past = blk < own
    rows = []
    for h in range(MOBA_HEADS):
        sl_h = slice(h * LANES, (h + 1) * LANES)
        gate = _dot_nt(kbar_ref[:, sl_h], qmf_ref[:, sl_h], precision=lax.Precision.HIGHEST)
        gate = jnp.where(past, gate, NEG_INF)
        ahead = jnp.zeros((n_blocks, tm), jnp.int32)
        for m in range(n_blocks):
            gm = gate[m:m + 1, :]
            beats = (gm > gate) | ((gm == gate) & (m < blk))
            ahead = ahead + beats.astype(jnp.int32)
        rows.append(jnp.where(past & (ahead < MOBA_TOPK), 1.0, 0.0).astype(F32))
    pad = jnp.zeros((LANES - MOBA_HEADS * n_blocks, tm), F32)
    sel_ref[...] = jnp.concatenate(rows + [pad], axis=0).T


def _attn_prep(x2, g_attn, w_in_p, g_q, w_uq_p, g_kv, w_uk, w_uv, tabs, *, seq):
    T, D = x2.shape
    tm = 512
    n_blocks = seq // MOBA_BLOCK
    assert seq % tm == 0 and tm % MOBA_BLOCK == 0 and MOBA_HEADS * n_blocks <= LANES
    cm, sm, cl, sl = tabs
    row = lambda w: pl.BlockSpec((tm, w), lambda i: (i, 0))
    kern = functools.partial(_attn_prep_kernel, tm=tm, tiles_per_seq=seq // tm, n_blocks=n_blocks)
    outs = [(MLA_HEADS * MLA_QK_PAD, BF16), (MLA_HEADS * MLA_QK_PAD, BF16), (MLA_HEADS * MLA_V, BF16),
            (MOBA_W, BF16), (MOBA_W, BF16), (MOBA_W, BF16), (LANES, F32)]
    return pl.pallas_call(
        kern,
        grid=(T // tm,),
        in_specs=[row(D), _const_spec(g_attn.shape), _const_spec(w_in_p.shape), _const_spec(g_q.shape),
                  _const_spec(w_uq_p.shape), _const_spec(g_kv.shape), _const_spec(w_uk.shape),
                  _const_spec(w_uv.shape), row(LANES), row(LANES), row(LANES), row(LANES)],
        out_specs=[row(w) for w, _ in outs],
        out_shape=[jax.ShapeDtypeStruct((T, w), dt) for w, dt in outs],
        scratch_shapes=[pltpu.VMEM((n_blocks, MOBA_W), F32), pltpu.VMEM((tm, MOBA_W), F32),
                        pltpu.VMEM((tm, MOBA_W), F32)],
        compiler_params=_params("arbitrary"),
        name="attn_prep",
    )(x2, g_attn, w_in_p, g_q, w_uq_p, g_kv, w_uk, w_uv, cm, sm, cl, sl)


ATTN_TQ = 256


def _softmax_av(s, v):
    m = jnp.max(s, axis=-1, keepdims=True)
    e = jnp.exp(s - m)
    l = jnp.sum(e, axis=-1, keepdims=True)
    return _dot(e.astype(BF16), v) / l


def _causal_bias(tq):
    r = lax.broadcasted_iota(jnp.int32, (tq, tq), 0)
    c = lax.broadcasted_iota(jnp.int32, (tq, tq), 1)
    return jnp.where(c <= r, 0.0, NEG_INF).astype(F32)


def _mla_attn_kernel(q_ref, k_ref, v_ref, o_ref, *, seq, scale):
    tq = ATTN_TQ
    diag = _causal_bias(tq)
    for i in range(seq // tq):
        n = (i + 1) * tq
        s = _dot_nt(q_ref[i * tq:n, :], k_ref[0:n, :]) * scale
        if i:
            bias = jnp.concatenate([jnp.zeros((tq, i * tq), F32), diag], axis=1)
        else:
            bias = diag
        o_ref[i * tq:n, :] = _softmax_av(s + bias, v_ref[0:n, :])


def _mla_attention(ql, kl, vl, *, batch, seq):
    T = ql.shape[0]
    scale = (MLA_NOPE + MLA_ROPE) ** -0.5
    return pl.pallas_call(
        functools.partial(_mla_attn_kernel, seq=seq, scale=scale),
        grid=(batch, MLA_HEADS),
        in_specs=[pl.BlockSpec((seq, MLA_QK_PAD), lambda b, h: (b, h)),
                  pl.BlockSpec((seq, MLA_QK_PAD), lambda b, h: (b, h)),
                  pl.BlockSpec((seq, MLA_V), lambda b, h: (b, h))],
        out_specs=pl.BlockSpec((seq, MLA_V), lambda b, h: (b, h)),
        out_shape=jax.ShapeDtypeStruct((T, MLA_HEADS * MLA_V), F32),
        compiler_params=_params("parallel", "parallel"),
        name="mla_attention",
    )(ql, kl, vl)


def _moba_attn_kernel(q_ref, k_ref, v_ref, sel_ref, o_ref, *, seq, scale, n_blocks):
    tq = ATTN_TQ
    h = pl.program_id(1)
    lane = lax.broadcasted_iota(jnp.int32, (1, LANES), 1)
    diag = _causal_bias(tq)
    for i in range(seq // tq):
        n = (i + 1) * tq
        s = _dot_nt(q_ref[i * tq:n, :], k_ref[0:n, :]) * scale
        sel = sel_ref[i * tq:n, :]
        pieces = []
        for b in range(i):
            chosen = jnp.sum(jnp.where(lane == h * n_blocks + b, sel, 0.0), axis=-1, keepdims=True)
            pieces.append(jnp.broadcast_to(jnp.where(chosen > 0.0, 0.0, NEG_INF), (tq, tq)))
        bias = jnp.concatenate(pieces + [diag], axis=1) if pieces else diag
        o_ref[i * tq:n, :] = _softmax_av(s + bias, v_ref[0:n, :])


def _moba_attention(qm, km, vm, sel, *, batch, seq):
    T = qm.shape[0]
    assert ATTN_TQ == MOBA_BLOCK
    hd = pl.BlockSpec((seq, MOBA_HEAD_DIM), lambda b, h: (b, h))
    return pl.pallas_call(
        functools.partial(_moba_attn_kernel, seq=seq, scale=MOBA_HEAD_DIM ** -0.5,
                          n_blocks=seq // MOBA_BLOCK),
        grid=(batch, MOBA_HEADS),
        in_specs=[hd, hd, hd, pl.BlockSpec((seq, LANES), lambda b, h: (b, 0))],
        out_specs=hd,
        out_shape=jax.ShapeDtypeStruct((T, MOBA_W), F32),
        compiler_params=_params("parallel", "parallel"),
        name="moba_attention",
    )(qm, km, vm, sel)


def _out_proj_kernel(x_ref, ol_ref, om_ref, gl_ref, gm_ref, w_ref, h_ref, *, wl):
    a = _rms(ol_ref[...], gl_ref[...]).astype(BF16)
    b = _rms(om_ref[...], gm_ref[...]).astype(BF16)
    h_ref[...] = x_ref[...] + (_dot(a, w_ref[0:wl, :]) + _dot(b, w_ref[wl:, :]))


def _out_proj(x2, o_mla, o_moba, g_l, g_m, w_out):
    T, D = x2.shape
    tm = 512
    row = lambda w: pl.BlockSpec((tm, w), lambda i: (i, 0))
    return pl.pallas_call(
        functools.partial(_out_proj_kernel, wl=o_mla.shape[1]),
        grid=(T // tm,),
        in_specs=[row(D), row(o_mla.shape[1]), row(o_moba.shape[1]), _const_spec(g_l.shape),
                  _const_spec(g_m.shape), _const_spec(w_out.shape)],
        out_specs=row(D),
        out_shape=jax.ShapeDtypeStruct((T, D), F32),
        compiler_params=_params("parallel"),
        name="out_proj",
    )(x2, o_mla, o_moba, g_l, g_m, w_out)


def _rms_matmul_kernel(x_ref, g_ref, w_ref, o_ref):
    xn = _rms(x_ref[...], g_ref[...]).astype(BF16)
    o_ref[...] = _dot(xn, w_ref[...]).astype(o_ref.dtype)


def _rms_matmul(x, g, w, out_dtype, *, tm, tn):
    M, K = x.shape
    N = w.shape[1]
    return pl.pallas_call(
        _rms_matmul_kernel,
        grid=(M // tm, N // tn),
        in_specs=[pl.BlockSpec((tm, K), lambda i, j: (i, 0)), pl.BlockSpec((1, K), lambda i, j: (0, 0)),
                  pl.BlockSpec((K, tn), lambda i, j: (0, j))],
        out_specs=pl.BlockSpec((tm, tn), lambda i, j: (i, j)),
        out_shape=jax.ShapeDtypeStruct((M, N), out_dtype),
        compiler_params=_params("parallel", "parallel"),
        name="rms_matmul",
    )(x, g, w)


def _cross_kernel(h_ref, g_ref, wq_ref, kv_ref, wo_ref, o_ref, *, scale):
    h = h_ref[...]
    D = h.shape[1]
    dh = D // CROSS_HEADS
    q = _dot(_rms(h, g_ref[...]).astype(BF16), wq_ref[...]).astype(BF16)
    outs = []
    for hd in range(CROSS_HEADS):
        s = _dot_nt(q[:, hd * dh:(hd + 1) * dh], kv_ref[:, hd * dh:(hd + 1) * dh]) * scale
        outs.append(_softmax_av(s, kv_ref[:, D + hd * dh:D + (hd + 1) * dh]).astype(BF16))
    o_ref[...] = h + _dot(jnp.concatenate(outs, axis=1), wo_ref[...])


def _cross_attention(h1, g_cross, w_cq, kv, w_co, *, seq, mem_len):
    T, D = h1.shape
    tm = 256
    tiles_per_seq = seq // tm
    row = pl.BlockSpec((tm, D), lambda i: (i, 0))
    return pl.pallas_call(
        functools.partial(_cross_kernel, scale=(D // CROSS_HEADS) ** -0.5),
        grid=(T // tm,),
        in_specs=[row, _const_spec(g_cross.shape), _const_spec(w_cq.shape),
                  pl.BlockSpec((mem_len, 2 * D), lambda i: (i // tiles_per_seq, 0)),
                  _const_spec(w_co.shape)],
        out_specs=row,
        out_shape=jax.ShapeDtypeStruct((T, D), F32),
        compiler_params=_params("parallel"),
        name="cross_attention",
    )(h1, g_cross, w_cq, kv, w_co)


def _top16_cols(s):
    n, tm = s.shape
    idx = lax.broadcasted_iota(jnp.int32, (n, tm), 0)
    rank = jnp.full((n, tm), PEER_TOPK, jnp.int32)
    vals = []
    for r in range(PEER_TOPK):
        m = jnp.max(s, axis=0, keepdims=True)
        first = jnp.min(jnp.where(s == m, idx, n), axis=0, keepdims=True)
        hit = idx == first
        rank = jnp.where(hit, r, rank)
        s = jnp.where(hit, NEG_INF, s)
        vals.append(m)
    return vals, rank


_PAIR_B = [PEER_TOPK // (a + 1) for a in range(8)]


def _pair_counts(v1, v2):
    tm = v1[0].shape[1]
    v1s = jnp.concatenate(v1, axis=0)
    v2s = jnp.concatenate(v2, axis=0)
    pieces, keys = [], []
    for a, nb in enumerate(_PAIR_B):
        rows = -(-nb // 8) * 8
        r = lax.broadcasted_iota(jnp.int32, (rows, tm), 0)
        pieces.append(jnp.where(r < nb, v2s[0:rows, :] + v1[a], NEG_INF))
        keys.append(a * PEER_TOPK + r)
    r = lax.broadcasted_iota(jnp.int32, (8, tm), 0)
    pieces.append(v1s[8:16, :] + v2[0])
    keys.append((r + 8) * PEER_TOPK)
    cand = jnp.concatenate(pieces, axis=0)
    key = jnp.concatenate(keys, axis=0)
    big = PEER_TOPK * PEER_TOPK
    work = cand
    chosen = jnp.zeros(cand.shape, jnp.bool_)
    for _ in range(PEER_TOPK):
        m = jnp.max(work, axis=0, keepdims=True)
        first = jnp.min(jnp.where(work == m, key, big), axis=0, keepdims=True)
        hit = key == first
        chosen = chosen | hit
        work = jnp.where(hit, NEG_INF, work)
    top = v1[0] + v2[0]
    z = jnp.sum(jnp.where(chosen, jnp.exp(cand - top), 0.0), axis=0, keepdims=True)
    picked = chosen.astype(F32)
    counts, o = [], 0
    for nb in _PAIR_B:
        rows = -(-nb // 8) * 8
        counts.append(jnp.sum(picked[o:o + rows, :], axis=0, keepdims=True))
        o += rows
    for a in range(8):
        counts.append(picked[o + a:o + a + 1, :])
    return counts, z


Z_E1, Z_CNT, Z_RANK2, Z_E2 = range(4)
Z_GROUPS = PEER_NKEYS // 8


def _peer_select_kernel(h_ref, g_ref, wq_ref, k1_ref, k2_ref, xn_ref, z_ref):
    xn = _rms(h_ref[...], g_ref[...]).astype(BF16)
    xn_ref[...] = xn
    q = _dot(xn, wq_ref[...])
    half = PEER_DKEY // 2
    hi = lax.Precision.HIGHEST
    for hd in range(PEER_HEADS):
        o = hd * PEER_DKEY
        s1 = _dot_nt(k1_ref[...], q[:, o:o + half], precision=hi)
        s2 = _dot_nt(k2_ref[...], q[:, o + half:o + 2 * half], precision=hi)
        v1, rank1 = _top16_cols(s1)
        v2, rank2 = _top16_cols(s2)
        counts, z = _pair_counts(v1, v2)
        cnt = jnp.zeros(s1.shape, F32)
        for a in range(PEER_TOPK):
            cnt = jnp.where(rank1 == a, counts[a], cnt)
        planes = {Z_E1: jnp.exp(s1 - v1[0]), Z_CNT: cnt, Z_RANK2: rank2.astype(F32),
                  Z_E2: jnp.exp(s2 - v2[0]) / z}
        for c in range(s1.shape[1] // LANES):
            for p, val in planes.items():
                z_ref[c, hd, :, p] = val[:, c * LANES:(c + 1) * LANES].reshape(Z_GROUPS, 8, LANES)


def _peer_select(h2, g_ffn, w_pq, k1, k2):
    T, D = h2.shape
    tm = 256
    row = pl.BlockSpec((tm, D), lambda i: (i, 0))
    z_block = (tm // LANES, PEER_HEADS, Z_GROUPS, 4, 8, LANES)
    return pl.pallas_call(
        _peer_select_kernel,
        grid=(T // tm,),
        in_specs=[row, _const_spec(g_ffn.shape), _const_spec(w_pq.shape), _const_spec(k1.shape),
                  _const_spec(k2.shape)],
        out_specs=[row, pl.BlockSpec(z_block, lambda i: (i, 0, 0, 0, 0, 0))],
        out_shape=[jax.ShapeDtypeStruct((T, D), BF16),
                   jax.ShapeDtypeStruct((T // LANES,) + z_block[1:], F32)],
        compiler_params=_params("parallel"),
        name="peer_select",
    )(h2, g_ffn, w_pq, k1, k2)


GATE_ROWS = 32
MXU_COLS = 256


def _peer_gate_lanes(at_ref, wt_ref, z_ref, group, sub0, c, *, te):
    n_i1 = te // PEER_NKEYS
    groups_per_block = GATE_ROWS // 8
    for sg in range(PEER_NKEYS // GATE_ROWS):
        gs = slice(sg * groups_per_block, (sg + 1) * groups_per_block)
        gate = [None] * n_i1
        for hd in range(PEER_HEADS):
            rank2 = z_ref[c, hd, gs, Z_RANK2]
            e2 = z_ref[c, hd, gs, Z_E2]
            for r in range(n_i1):
                rs = slice(sub0 + r, sub0 + r + 1)
                e1 = z_ref[c, hd, group, Z_E1, rs, :]
                cnt = z_ref[c, hd, group, Z_CNT, rs, :]
                term = e1 * jnp.where(rank2 < cnt, e2, 0.0)
                gate[r] = term if gate[r] is None else gate[r] + term
        for r in range(n_i1):
            es = slice(r * PEER_NKEYS + sg * GATE_ROWS, r * PEER_NKEYS + (sg + 1) * GATE_ROWS)
            a = at_ref[c, es, :]
            act = 0.5 * a * (1.0 + lax.erf(a * math.sqrt(0.5)))
            wt_ref[c, es, :] = (gate[r].reshape(GATE_ROWS, LANES) * act).astype(BF16)


def _peer_dense_kernel(xn_ref, u_ref, vt_ref, z_ref, h_ref, gf_ref,
                       o_ref, acc_ref, at0_ref, at1_ref, wt0_ref, wt1_ref, *, te, tm, n_tiles):
    j = pl.program_id(1)

    @pl.when(j == 0)
    def _():
        acc_ref[...] = jnp.zeros_like(acc_ref)
        at1_ref[...] = jnp.zeros_like(at1_ref)
        wt0_ref[...] = jnp.zeros_like(wt0_ref)

    rows_per_step = te // PEER_NKEYS
    assert 2 * rows_per_step == 8 and n_tiles % 2 == 0

    n_chunks = tm // LANES
    assert n_chunks == 4 and 2 * LANES == MXU_COLS
    half_d = acc_ref.shape[1] // 2

    def step(at_w, at_r, wt_w, wt_r, tile_b, sub0):
        for c in range(n_chunks):
            p, dh = c % 2, slice((c // 2) * half_d, (c // 2 + 1) * half_d)
            w = jnp.concatenate([wt_r[2 * p], wt_r[2 * p + 1]], axis=1)
            res = _dot(vt_ref[dh, :], w)
            acc_ref[2 * p, dh, :] += res[:, :LANES]
            acc_ref[2 * p + 1, dh, :] += res[:, LANES:]
            _peer_gate_lanes(at_r, wt_w, z_ref, tile_b // 2, sub0, c, te=te)
            if c % 2 == 1:
                res = _dot_nt(u_ref[...], xn_ref[(c - 1) * LANES:(c + 1) * LANES, :])
                at_w[c - 1] = res[:, :LANES]
                at_w[c] = res[:, LANES:]

    @pl.when(j % 2 == 0)
    def _():
        step(at0_ref, at1_ref, wt1_ref, wt0_ref, jnp.maximum(j - 1, 1), rows_per_step)

    @pl.when(j % 2 == 1)
    def _():
        step(at1_ref, at0_ref, wt0_ref, wt1_ref, jnp.minimum(j - 1, n_tiles - 2), 0)

    @pl.when(j == pl.num_programs(1) - 1)
    def _():
        for c in range(n_chunks):
            rows = slice(c * LANES, (c + 1) * LANES)
            o_ref[rows, :] = _rms(h_ref[rows, :] + acc_ref[c].T, gf_ref[...])


def _peer_dense(xn, u, vt, z, h2, g_final):
    T, D = h2.shape
    E = u.shape[0]
    tm, te = 512, 512
    n_tiles = E // te
    n_chunks = tm // LANES
    clamp = lambda t: jnp.minimum(jnp.maximum(t, 0), n_tiles - 1)
    row = pl.BlockSpec((tm, D), lambda i, j: (i, 0))
    return pl.pallas_call(
        functools.partial(_peer_dense_kernel, te=te, tm=tm, n_tiles=n_tiles),
        grid=(T // tm, n_tiles + 2),
        in_specs=[row, pl.BlockSpec((te, D), lambda i, j: (clamp(j), 0)),
                  pl.BlockSpec((D, te), lambda i, j: (0, clamp(j - 2))),
                  pl.BlockSpec((n_chunks,) + z.shape[1:], lambda i, j: (i, 0, 0, 0, 0, 0)),
                  row, pl.BlockSpec((1, D), lambda i, j: (0, 0))],
        out_specs=row,
        out_shape=jax.ShapeDtypeStruct((T, D), F32),
        scratch_shapes=[pltpu.VMEM((n_chunks, D, LANES), F32),
                        pltpu.VMEM((n_chunks, te, LANES), F32), pltpu.VMEM((n_chunks, te, LANES), F32),
                        pltpu.VMEM((n_chunks, te, LANES), BF16), pltpu.VMEM((n_chunks, te, LANES), BF16)],
        compiler_params=_params("parallel", "arbitrary"),
        name="peer_dense",
    )(xn, u, vt, z, h2, g_final)


def _layer(h, mem2, tabs, g_attn, w_in, g_q_a, w_uq, g_kv_a, w_ukv, g_mla_o, g_moba_o, w_out, g_cross,
           g_mem, w_cq, w_ck, w_cv, w_co, g_ffn, w_pq, sub_keys1, sub_keys2, expert_u, expert_v,
           g_final, *, batch, seq, mem_len):
    D = h.shape[1]
    row = lambda g: g.reshape(1, -1)
    n_lat = MLA_Q_RANK + MLA_KV_RANK + MLA_ROPE
    w_in_p = jnp.concatenate([w_in[:, :n_lat], jnp.zeros((D, LANES - MLA_ROPE), w_in.dtype),
                              w_in[:, n_lat:]], axis=1).astype(BF16)
    w_uq_p = jnp.pad(w_uq.reshape(MLA_Q_RANK, MLA_HEADS, MLA_NOPE + MLA_ROPE),
                     ((0, 0), (0, 0), (0, MLA_QK_PAD - MLA_NOPE - MLA_ROPE))
                     ).reshape(MLA_Q_RANK, MLA_HEADS * MLA_QK_PAD).astype(BF16)
    w_ukv3 = w_ukv.reshape(MLA_KV_RANK, MLA_HEADS, MLA_NOPE + MLA_V)
    w_uk = w_ukv3[:, :, :MLA_NOPE].reshape(MLA_KV_RANK, MLA_HEADS * MLA_NOPE).astype(BF16)
    w_uv = w_ukv3[:, :, MLA_NOPE:].reshape(MLA_KV_RANK, MLA_HEADS * MLA_V).astype(BF16)

    ql, kl, vl, qm, km, vm, sel = _attn_prep(h, row(g_attn), w_in_p, row(g_q_a), w_uq_p, row(g_kv_a),
                                             w_uk, w_uv, tabs, seq=seq)
    o_mla = _mla_attention(ql, kl, vl, batch=batch, seq=seq)
    o_moba = _moba_attention(qm, km, vm, sel, batch=batch, seq=seq)
    h1 = _out_proj(h, o_mla, o_moba, row(g_mla_o), row(g_moba_o), w_out.astype(BF16))

    w_ckv = jnp.concatenate([w_ck, w_cv], axis=1).astype(BF16)
    kv = _rms_matmul(mem2, row(g_mem), w_ckv, BF16, tm=min(mem2.shape[0], 512), tn=1024)
    h2 = _cross_attention(h1, row(g_cross), w_cq.astype(BF16), kv, w_co.astype(BF16), seq=seq, mem_len=mem_len)

    xn, z = _peer_select(h2, row(g_ffn), w_pq.astype(BF16), sub_keys1, sub_keys2)
    return _peer_dense(xn, expert_u.astype(BF16), expert_v.astype(BF16).T, z, h2, row(g_final))


def kernel(x, mem, positions, g_attn, w_in, g_q_a, w_uq, g_kv_a, w_ukv, g_mla_o, g_moba_o, w_out, g_cross,
           g_mem, w_cq, w_ck, w_cv, w_co, g_ffn, w_pq, sub_keys1, sub_keys2, expert_u, expert_v, g_final):
    B, S, D = x.shape
    M = mem.shape[1]
    depth = w_in.shape[0]
    assert depth == 1, "the final norm is fused into the layer's last kernel"
    tabs = _rope_tables(positions.reshape(B * S, 1))
    out = _layer(x.reshape(B * S, D), mem.reshape(B * M, D), tabs, g_attn[0], w_in[0], g_q_a[0], w_uq[0],
                 g_kv_a[0], w_ukv[0], g_mla_o[0], g_moba_o[0], w_out[0], g_cross[0], g_mem[0], w_cq[0],
                 w_ck[0], w_cv[0], w_co[0], g_ffn[0], w_pq[0], sub_keys1[0], sub_keys2[0], expert_u[0],
                 expert_v[0], g_final, batch=B, seq=S, mem_len=M)
    return out.reshape(B, S, D)
```

```python
import functools
import math

import jax
import jax.numpy as jnp
from jax import lax
from jax.experimental import pallas as pl
from jax.experimental.pallas import tpu as pltpu

F32 = jnp.float32
BF16 = jnp.bfloat16

EPS = 1e-6
ROPE_THETA = 10000.0
MLA_HEADS = 8
MLA_Q_RANK = 512
MLA_KV_RANK = 256
MLA_NOPE = 128
MLA_ROPE = 64
MLA_V = 128
MLA_QK_PAD = 256
MOBA_HEADS = 8
MOBA_HEAD_DIM = 128
MOBA_BLOCK = 256
MOBA_TOPK = 3
CROSS_HEADS = 4
PEER_HEADS = 8
PEER_NKEYS = 128
PEER_DKEY = 256
PEER_TOPK = 16

LANES = 128
VMEM_LIMIT = 56 * 1024 * 1024

NEG_INF = float("-inf")
NT_DIMS = (((1,), (1,)), ((), ()))


def _rms(x, g):
    ms = jnp.mean(x * x, axis=-1, keepdims=True)
    return x * lax.rsqrt(ms + EPS) * g


def _dot(a, b):
    return jnp.dot(a, b, preferred_element_type=F32)


def _dot_nt(a, b, precision=None):
    return lax.dot_general(a, b, NT_DIMS, precision=precision, preferred_element_type=F32)


def _const_spec(shape):
    nd = len(shape)
    return pl.BlockSpec(shape, lambda *_: (0,) * nd, pipeline_mode=pl.Buffered(1))


def _params(*sem, flags=None):
    return pltpu.CompilerParams(dimension_semantics=sem, vmem_limit_bytes=VMEM_LIMIT, flags=flags)


def _rope_table_kernel(pos_ref, cm_ref, sm_ref, cl_ref, sl_ref):
    pos = pos_ref[...].astype(F32)
    lane = lax.broadcasted_iota(jnp.int32, (1, LANES), 1)
    half = MOBA_HEAD_DIM // 2
    inv = jnp.exp((lane % half).astype(F32) * (-2.0 * math.log(ROPE_THETA) / MOBA_HEAD_DIM))
    ang = pos * inv
    cm_ref[...] = jnp.cos(ang)
    sn = jnp.sin(ang)
    sm_ref[...] = jnp.where(lane < half, -sn, sn)
    half = MLA_ROPE // 2
    inv = jnp.exp((lane % half).astype(F32) * (-2.0 * math.log(ROPE_THETA) / MLA_ROPE))
    ang = pos * inv
    valid = lane < MLA_ROPE
    cl_ref[...] = jnp.where(valid, jnp.cos(ang), 0.0)
    sn = jnp.sin(ang)
    sl_ref[...] = jnp.where(lane < half, -sn, jnp.where(valid, sn, 0.0))


def _rope_tables(pos_col):
    T = pos_col.shape[0]
    tm = 512
    out = jax.ShapeDtypeStruct((T, LANES), F32)
    spec = pl.BlockSpec((tm, LANES), lambda i: (i, 0))
    return pl.pallas_call(
        _rope_table_kernel,
        grid=(T // tm,),
        in_specs=[pl.BlockSpec((tm, 1), lambda i: (i, 0))],
        out_specs=[spec] * 4,
        out_shape=[out] * 4,
        compiler_params=_params("parallel"),
        name="rope_tables",
    )(pos_col)


def _rope_moba(x, cos, sin):
    return x * cos + pltpu.roll(x, MOBA_HEAD_DIM // 2, 1) * sin


def _rope_mla(x, cos, sin, lane):
    half = MLA_ROPE // 2
    partner = jnp.where(lane < half, pltpu.roll(x, LANES - half, 1), pltpu.roll(x, half, 1))
    return x * cos + partner * sin


C_COLS = MLA_Q_RANK + MLA_KV_RANK + LANES
MOBA_W = MOBA_HEADS * MOBA_HEAD_DIM


def _attn_prep_kernel(x_ref, g_ref, w_in_ref, gq_ref, wuq_ref, gkv_ref, wuk_ref, wuv_ref,
                      cm_ref, sm_ref, cl_ref, sl_ref,
                      ql_ref, kl_ref, vl_ref, qm_ref, km_ref, vm_ref, sel_ref,
                      kbar_ref, qmf_ref, kmf_ref, *, tm, tiles_per_seq, n_blocks):
    i = pl.program_id(0)

    @pl.when(i == 0)
    def _():
        kbar_ref[...] = jnp.zeros_like(kbar_ref)

    xn = _rms(x_ref[...], g_ref[...]).astype(BF16)
    lane = lax.broadcasted_iota(jnp.int32, (1, LANES), 1)
    cl, sl = cl_ref[...], sl_ref[...]
    cm, sm = cm_ref[...], sm_ref[...]

    c = _dot(xn, w_in_ref[:, 0:C_COLS])
    cq = _rms(c[:, 0:MLA_Q_RANK], gq_ref[...]).astype(BF16)
    q = _dot(cq, wuq_ref[...])
    ckv = _rms(c[:, MLA_Q_RANK:MLA_Q_RANK + MLA_KV_RANK], gkv_ref[...]).astype(BF16)
    kn = _dot(ckv, wuk_ref[...])
    vl_ref[...] = _dot(ckv, wuv_ref[...]).astype(BF16)
    kr = _rope_mla(c[:, MLA_Q_RANK + MLA_KV_RANK:C_COLS], cl, sl, lane).astype(BF16)
    for h in range(MLA_HEADS):
        o = h * MLA_QK_PAD
        ql_ref[:, o:o + LANES] = q[:, o:o + LANES].astype(BF16)
        ql_ref[:, o + LANES:o + 2 * LANES] = _rope_mla(q[:, o + LANES:o + 2 * LANES], cl, sl, lane).astype(BF16)
        kl_ref[:, o:o + LANES] = kn[:, h * LANES:(h + 1) * LANES].astype(BF16)
        kl_ref[:, o + LANES:o + 2 * LANES] = kr

    o0 = C_COLS
    qm = _dot(xn, w_in_ref[:, o0:o0 + MOBA_W])
    km = _dot(xn, w_in_ref[:, o0 + MOBA_W:o0 + 2 * MOBA_W])
    vm_ref[...] = _dot(xn, w_in_ref[:, o0 + 2 * MOBA_W:o0 + 3 * MOBA_W]).astype(BF16)
    for h in range(MOBA_HEADS):
        sl_h = slice(h * LANES, (h + 1) * LANES)
        qr = _rope_moba(qm[:, sl_h], cm, sm)
        kr_h = _rope_moba(km[:, sl_h], cm, sm)
        qmf_ref[:, sl_h] = qr
        kmf_ref[:, sl_h] = kr_h
        qm_ref[:, sl_h] = qr.astype(BF16)
        km_ref[:, sl_h] = kr_h.astype(BF16)

    blk0 = (i % tiles_per_seq) * (tm // MOBA_BLOCK)
    for s in range(tm // MOBA_BLOCK):
        kbar_ref[pl.ds(blk0 + s, 1), :] = jnp.mean(
            kmf_ref[s * MOBA_BLOCK:(s + 1) * MOBA_BLOCK, :], axis=0, keepdims=True)
    tok = lax.broadcasted_iota(jnp.int32, (n_blocks, tm), 1)
    own = blk0 + tok // MOBA_BLOCK
    blk = lax.broadcasted_iota(jnp.int32, (n_blocks, tm), 0)
    past = blk < own
    rows = []
    for h in range(MOBA_HEADS):
        sl_h = slice(h * LANES, (h + 1) * LANES)
        gate = _dot_nt(kbar_ref[:, sl_h], qmf_ref[:, sl_h], precision=lax.Precision.HIGHEST)
        gate = jnp.where(past, gate, NEG_INF)
        ahead = jnp.zeros((n_blocks, tm), jnp.int32)
        for m in range(n_blocks):
            gm = gate[m:m + 1, :]
            beats = (gm > gate) | ((gm == gate) & (m < blk))
            ahead = ahead + beats.astype(jnp.int32)
        rows.append(jnp.where(past & (ahead < MOBA_TOPK), 1.0, 0.0).astype(F32))
    pad = jnp.zeros((LANES - MOBA_HEADS * n_blocks, tm), F32)
    sel_ref[...] = jnp.concatenate(rows + [pad], axis=0).T


def _attn_prep(x2, g_attn, w_in_p, g_q, w_uq_p, g_kv, w_uk, w_uv, tabs, *, seq):
    T, D = x2.shape
    tm = 512
    n_blocks = seq // MOBA_BLOCK
    assert seq % tm == 0 and tm % MOBA_BLOCK == 0 and MOBA_HEADS * n_blocks <= LANES
    cm, sm, cl, sl = tabs
    row = lambda w: pl.BlockSpec((tm, w), lambda i: (i, 0))
    kern = functools.partial(_attn_prep_kernel, tm=tm, tiles_per_seq=seq // tm, n_blocks=n_blocks)
    outs = [(MLA_HEADS * MLA_QK_PAD, BF16), (MLA_HEADS * MLA_QK_PAD, BF16), (MLA_HEADS * MLA_V, BF16),
            (MOBA_W, BF16), (MOBA_W, BF16), (MOBA_W, BF16), (LANES, F32)]
    return pl.pallas_call(
        kern,
        grid=(T // tm,),
        in_specs=[row(D), _const_spec(g_attn.shape), _const_spec(w_in_p.shape), _const_spec(g_q.shape),
                  _const_spec(w_uq_p.shape), _const_spec(g_kv.shape), _const_spec(w_uk.shape),
                  _const_spec(w_uv.shape), row(LANES), row(LANES), row(LANES), row(LANES)],
        out_specs=[row(w) for w, _ in outs],
        out_shape=[jax.ShapeDtypeStruct((T, w), dt) for w, dt in outs],
        scratch_shapes=[pltpu.VMEM((n_blocks, MOBA_W), F32), pltpu.VMEM((tm, MOBA_W), F32),
                        pltpu.VMEM((tm, MOBA_W), F32)],
        compiler_params=_params("arbitrary"),
        name="attn_prep",
    )(x2, g_attn, w_in_p, g_q, w_uq_p, g_kv, w_uk, w_uv, cm, sm, cl, sl)


ATTN_TQ = 256


def _softmax_av(s, v):
    m = jnp.max(s, axis=-1, keepdims=True)
    e = jnp.exp(s - m)
    l = jnp.sum(e, axis=-1, keepdims=True)
    return _dot(e.astype(BF16), v) / l


def _causal_bias(tq):
    r = lax.broadcasted_iota(jnp.int32, (tq, tq), 0)
    c = lax.broadcasted_iota(jnp.int32, (tq, tq), 1)
    return jnp.where(c <= r, 0.0, NEG_INF).astype(F32)


def _mla_attn_kernel(q_ref, k_ref, v_ref, o_ref, *, seq, scale):
    tq = ATTN_TQ
    diag = _causal_bias(tq)
    for i in range(seq // tq):
        n = (i + 1) * tq
        s = _dot_nt(q_ref[i * tq:n, :], k_ref[0:n, :]) * scale
        if i:
            bias = jnp.concatenate([jnp.zeros((tq, i * tq), F32), diag], axis=1)
        else:
            bias = diag
        o_ref[i * tq:n, :] = _softmax_av(s + bias, v_ref[0:n, :])


def _mla_attention(ql, kl, vl, *, batch, seq):
    T = ql.shape[0]
    scale = (MLA_NOPE + MLA_ROPE) ** -0.5
    return pl.pallas_call(
        functools.partial(_mla_attn_kernel, seq=seq, scale=scale),
        grid=(batch, MLA_HEADS),
        in_specs=[pl.BlockSpec((seq, MLA_QK_PAD), lambda b, h: (b, h)),
                  pl.BlockSpec((seq, MLA_QK_PAD), lambda b, h: (b, h)),
                  pl.BlockSpec((seq, MLA_V), lambda b, h: (b, h))],
        out_specs=pl.BlockSpec((seq, MLA_V), lambda b, h: (b, h)),
        out_shape=jax.ShapeDtypeStruct((T, MLA_HEADS * MLA_V), F32),
        compiler_params=_params("parallel", "parallel"),
        name="mla_attention",
    )(ql, kl, vl)


def _moba_attn_kernel(q_ref, k_ref, v_ref, sel_ref, o_ref, *, seq, scale, n_blocks):
    tq = ATTN_TQ
    h = pl.program_id(1)
    lane = lax.broadcasted_iota(jnp.int32, (1, LANES), 1)
    diag = _causal_bias(tq)
    for i in range(seq // tq):
        n = (i + 1) * tq
        s = _dot_nt(q_ref[i * tq:n, :], k_ref[0:n, :]) * scale
        sel = sel_ref[i * tq:n, :]
        pieces = []
        for b in range(i):
            chosen = jnp.sum(jnp.where(lane == h * n_blocks + b, sel, 0.0), axis=-1, keepdims=True)
            pieces.append(jnp.broadcast_to(jnp.where(chosen > 0.0, 0.0, NEG_INF), (tq, tq)))
        bias = jnp.concatenate(pieces + [diag], axis=1) if pieces else diag
        o_ref[i * tq:n, :] = _softmax_av(s + bias, v_ref[0:n, :])


def _moba_attention(qm, km, vm, sel, *, batch, seq):
    T = qm.shape[0]
    assert ATTN_TQ == MOBA_BLOCK
    hd = pl.BlockSpec((seq, MOBA_HEAD_DIM), lambda b, h: (b, h))
    return pl.pallas_call(
        functools.partial(_moba_attn_kernel, seq=seq, scale=MOBA_HEAD_DIM ** -0.5,
                          n_blocks=seq // MOBA_BLOCK),
        grid=(batch, MOBA_HEADS),
        in_specs=[hd, hd, hd, pl.BlockSpec((seq, LANES), lambda b, h: (b, 0))],
        out_specs=hd,
        out_shape=jax.ShapeDtypeStruct((T, MOBA_W), F32),
        compiler_params=_params("parallel", "parallel"),
        name="moba_attention",
    )(qm, km, vm, sel)


def _out_proj_kernel(x_ref, ol_ref, om_ref, gl_ref, gm_ref, w_ref, h_ref, *, wl):
    a = _rms(ol_ref[...], gl_ref[...]).astype(BF16)
    b = _rms(om_ref[...], gm_ref[...]).astype(BF16)
    h_ref[...] = x_ref[...] + (_dot(a, w_ref[0:wl, :]) + _dot(b, w_ref[wl:, :]))


def _out_proj(x2, o_mla, o_moba, g_l, g_m, w_out):
    T, D = x2.shape
    tm = 512
    row = lambda w: pl.BlockSpec((tm, w), lambda i: (i, 0))
    return pl.pallas_call(
        functools.partial(_out_proj_kernel, wl=o_mla.shape[1]),
        grid=(T // tm,),
        in_specs=[row(D), row(o_mla.shape[1]), row(o_moba.shape[1]), _const_spec(g_l.shape),
                  _const_spec(g_m.shape), _const_spec(w_out.shape)],
        out_specs=row(D),
        out_shape=jax.ShapeDtypeStruct((T, D), F32),
        compiler_params=_params("parallel"),
        name="out_proj",
    )(x2, o_mla, o_moba, g_l, g_m, w_out)


def _rms_matmul_kernel(x_ref, g_ref, w_ref, o_ref):
    xn = _rms(x_ref[...], g_ref[...]).astype(BF16)
    o_ref[...] = _dot(xn, w_ref[...]).astype(o_ref.dtype)


def _rms_matmul(x, g, w, out_dtype, *, tm, tn):
    M, K = x.shape
    N = w.shape[1]
    return pl.pallas_call(
        _rms_matmul_kernel,
        grid=(M // tm, N // tn),
        in_specs=[pl.BlockSpec((tm, K), lambda i, j: (i, 0)), pl.BlockSpec((1, K), lambda i, j: (0, 0)),
                  pl.BlockSpec((K, tn), lambda i, j: (0, j))],
        out_specs=pl.BlockSpec((tm, tn), lambda i, j: (i, j)),
        out_shape=jax.ShapeDtypeStruct((M, N), out_dtype),
        compiler_params=_params("parallel", "parallel"),
        name="rms_matmul",
    )(x, g, w)


def _cross_kernel(h_ref, g_ref, wq_ref, kv_ref, wo_ref, o_ref, *, scale):
    h = h_ref[...]
    D = h.shape[1]
    dh = D // CROSS_HEADS
    q = _dot(_rms(h, g_ref[...]).astype(BF16), wq_ref[...]).astype(BF16)
    outs = []
    for hd in range(CROSS_HEADS):
        s = _dot_nt(q[:, hd * dh:(hd + 1) * dh], kv_ref[:, hd * dh:(hd + 1) * dh]) * scale
        outs.append(_softmax_av(s, kv_ref[:, D + hd * dh:D + (hd + 1) * dh]).astype(BF16))
    o_ref[...] = h + _dot(jnp.concatenate(outs, axis=1), wo_ref[...])


def _cross_attention(h1, g_cross, w_cq, kv, w_co, *, seq, mem_len):
    T, D = h1.shape
    tm = 256
    tiles_per_seq = seq // tm
    row = pl.BlockSpec((tm, D), lambda i: (i, 0))
    return pl.pallas_call(
        functools.partial(_cross_kernel, scale=(D // CROSS_HEADS) ** -0.5),
        grid=(T // tm,),
        in_specs=[row, _const_spec(g_cross.shape), _const_spec(w_cq.shape),
                  pl.BlockSpec((mem_len, 2 * D), lambda i: (i // tiles_per_seq, 0)),
                  _const_spec(w_co.shape)],
        out_specs=row,
        out_shape=jax.ShapeDtypeStruct((T, D), F32),
        compiler_params=_params("parallel"),
        name="cross_attention",
    )(h1, g_cross, w_cq, kv, w_co)


def _top16_cols(s):
    n, tm = s.shape
    idx = lax.broadcasted_iota(jnp.int32, (n, tm), 0)
    rank = jnp.full((n, tm), PEER_TOPK, jnp.int32)
    vals = []
    for r in range(PEER_TOPK):
        m = jnp.max(s, axis=0, keepdims=True)
        first = jnp.min(jnp.where(s == m, idx, n), axis=0, keepdims=True)
        hit = idx == first
        rank = jnp.where(hit, r, rank)
        s = jnp.where(hit, NEG_INF, s)
        vals.append(m)
    return vals, rank


_PAIR_B = [PEER_TOPK // (a + 1) for a in range(8)]


def _pair_counts(v1, v2):
    tm = v1[0].shape[1]
    v1s = jnp.concatenate(v1, axis=0)
    v2s = jnp.concatenate(v2, axis=0)
    pieces, keys = [], []
    for a, nb in enumerate(_PAIR_B):
        rows = -(-nb // 8) * 8
        r = lax.broadcasted_iota(jnp.int32, (rows, tm), 0)
        pieces.append(jnp.where(r < nb, v2s[0:rows, :] + v1[a], NEG_INF))
        keys.append(a * PEER_TOPK + r)
    r = lax.broadcasted_iota(jnp.int32, (8, tm), 0)
    pieces.append(v1s[8:16, :] + v2[0])
    keys.append((r + 8) * PEER_TOPK)
    cand = jnp.concatenate(pieces, axis=0)
    key = jnp.concatenate(keys, axis=0)
    big = PEER_TOPK * PEER_TOPK
    work = cand
    chosen = jnp.zeros(cand.shape, jnp.bool_)
    for _ in range(PEER_TOPK):
        m = jnp.max(work, axis=0, keepdims=True)
        first = jnp.min(jnp.where(work == m, key, big), axis=0, keepdims=True)
        hit = key == first
        chosen = chosen | hit
        work = jnp.where(hit, NEG_INF, work)
    top = v1[0] + v2[0]
    z = jnp.sum(jnp.where(chosen, jnp.exp(cand - top), 0.0), axis=0, keepdims=True)
    picked = chosen.astype(F32)
    counts, o = [], 0
    for nb in _PAIR_B:
        rows = -(-nb // 8) * 8
        counts.append(jnp.sum(picked[o:o + rows, :], axis=0, keepdims=True))
        o += rows
    for a in range(8):
        counts.append(picked[o + a:o + a + 1, :])
    return counts, z


Z_E1, Z_CNT, Z_RANK2, Z_E2 = range(4)
Z_GROUPS = PEER_NKEYS // 8


def _peer_select_kernel(h_ref, g_ref, wq_ref, k1_ref, k2_ref, xn_ref, z_ref):
    xn = _rms(h_ref[...], g_ref[...]).astype(BF16)
    xn_ref[...] = xn
    q = _dot(xn, wq_ref[...])
    half = PEER_DKEY // 2
    hi = lax.Precision.HIGHEST
    for hd in range(PEER_HEADS):
        o = hd * PEER_DKEY
        s1 = _dot_nt(k1_ref[...], q[:, o:o + half], precision=hi)
        s2 = _dot_nt(k2_ref[...], q[:, o + half:o + 2 * half], precision=hi)
        v1, rank1 = _top16_cols(s1)
        v2, rank2 = _top16_cols(s2)
        counts, z = _pair_counts(v1, v2)
        cnt = jnp.zeros(s1.shape, F32)
        for a in range(PEER_TOPK):
            cnt = jnp.where(rank1 == a, counts[a], cnt)
        planes = {Z_E1: jnp.exp(s1 - v1[0]), Z_CNT: cnt, Z_RANK2: rank2.astype(F32),
                  Z_E2: jnp.exp(s2 - v2[0]) / z}
        for c in range(s1.shape[1] // LANES):
            for p, val in planes.items():
                z_ref[c, hd, :, p] = val[:, c * LANES:(c + 1) * LANES].reshape(Z_GROUPS, 8, LANES)


def _peer_select(h2, g_ffn, w_pq, k1, k2):
    T, D = h2.shape
    tm = 256
    row = pl.BlockSpec((tm, D), lambda i: (i, 0))
    z_block = (tm // LANES, PEER_HEADS, Z_GROUPS, 4, 8, LANES)
    return pl.pallas_call(
        _peer_select_kernel,
        grid=(T // tm,),
        in_specs=[row, _const_spec(g_ffn.shape), _const_spec(w_pq.shape), _const_spec(k1.shape),
                  _const_spec(k2.shape)],
        out_specs=[row, pl.BlockSpec(z_block, lambda i: (i, 0, 0, 0, 0, 0))],
        out_shape=[jax.ShapeDtypeStruct((T, D), BF16),
                   jax.ShapeDtypeStruct((T // LANES,) + z_block[1:], F32)],
        compiler_params=_params("parallel"),
        name="peer_select",
    )(h2, g_ffn, w_pq, k1, k2)


GATE_ROWS = 32
MXU_COLS = 256


def _peer_gate_block(at_ref, wt_ref, z_ref, group, sub0, c, sg, *, te):
    n_i1 = te // PEER_NKEYS
    groups_per_block = GATE_ROWS // 8
    gs = slice(sg * groups_per_block, (sg + 1) * groups_per_block)
    gate = [None] * n_i1
    for hd in range(PEER_HEADS):
        rank2 = z_ref[c, hd, gs, Z_RANK2]
        e2 = z_ref[c, hd, gs, Z_E2]
        for r in range(n_i1):
            rs = slice(sub0 + r, sub0 + r + 1)
            e1 = z_ref[c, hd, group, Z_E1, rs, :]
            cnt = z_ref[c, hd, group, Z_CNT, rs, :]
            term = e1 * jnp.where(rank2 < cnt, e2, 0.0)
            gate[r] = term if gate[r] is None else gate[r] + term
    for r in range(n_i1):
        es = slice(r * PEER_NKEYS + sg * GATE_ROWS, r * PEER_NKEYS + (sg + 1) * GATE_ROWS)
        a = at_ref[c, es, :]
        act = 0.5 * a * (1.0 + lax.erf(a * math.sqrt(0.5)))
        wt_ref[c, es, :] = (gate[r].reshape(GATE_ROWS, LANES) * act).astype(BF16)


def _peer_dense_kernel(xn_ref, u_ref, vt_ref, z_ref, h_ref, gf_ref,
                       o_ref, acc_ref, at0_ref, at1_ref, wt0_ref, wt1_ref, *, te, tm, n_tiles):
    j = pl.program_id(1)

    @pl.when(j == 0)
    def _():
        acc_ref[...] = jnp.zeros_like(acc_ref)
        at1_ref[...] = jnp.zeros_like(at1_ref)
        wt0_ref[...] = jnp.zeros_like(wt0_ref)

    rows_per_step = te // PEER_NKEYS
    assert 2 * rows_per_step == 8 and n_tiles % 2 == 0

    n_chunks = tm // LANES
    assert n_chunks == 4 and 2 * LANES == MXU_COLS
    n_sg = PEER_NKEYS // GATE_ROWS
    d_piece = acc_ref.shape[1] // (n_chunks * n_sg // 2)

    def step(at_w, at_r, wt_w, wt_r, tile_b, sub0):
        for c in range(n_chunks):
            for sg in range(n_sg):
                k = c * n_sg + sg
                p, dr = k % 2, slice((k // 2) * d_piece, (k // 2 + 1) * d_piece)
                w = jnp.concatenate([wt_r[2 * p], wt_r[2 * p + 1]], axis=1)
                res = _dot(vt_ref[dr, :], w)
                acc_ref[2 * p, dr, :] += res[:, :LANES]
                acc_ref[2 * p + 1, dr, :] += res[:, LANES:]
                _peer_gate_block(at_r, wt_w, z_ref, tile_b // 2, sub0, c, sg, te=te)
            c0, er = 2 * (c // 2), slice((c % 2) * (te // 2), (c % 2 + 1) * (te // 2))
            res = _dot_nt(u_ref[er, :], xn_ref[c0 * LANES:(c0 + 2) * LANES, :])
            at_w[c0, er, :] = res[:, :LANES]
            at_w[c0 + 1, er, :] = res[:, LANES:]

    @pl.when(j % 2 == 0)
    def _():
        step(at0_ref, at1_ref, wt1_ref, wt0_ref, jnp.maximum(j - 1, 1), rows_per_step)

    @pl.when(j % 2 == 1)
    def _():
        step(at1_ref, at0_ref, wt0_ref, wt1_ref, jnp.minimum(j - 1, n_tiles - 2), 0)

    @pl.when(j == pl.num_programs(1) - 1)
    def _():
        for c in range(n_chunks):
            rows = slice(c * LANES, (c + 1) * LANES)
            o_ref[rows, :] = _rms(h_ref[rows, :] + acc_ref[c].T, gf_ref[...])


def _peer_dense(xn, u, vt, z, h2, g_final):
    T, D = h2.shape
    E = u.shape[0]
    tm, te = 512, 512
    n_tiles = E // te
    n_chunks = tm // LANES
    clamp = lambda t: jnp.minimum(jnp.maximum(t, 0), n_tiles - 1)
    row = pl.BlockSpec((tm, D), lambda i, j: (i, 0))
    return pl.pallas_call(
        functools.partial(_peer_dense_kernel, te=te, tm=tm, n_tiles=n_tiles),
        grid=(T // tm, n_tiles + 2),
        in_specs=[row, pl.BlockSpec((te, D), lambda i, j: (clamp(j), 0)),
                  pl.BlockSpec((D, te), lambda i, j: (0, clamp(j - 2))),
                  pl.BlockSpec((n_chunks,) + z.shape[1:], lambda i, j: (i, 0, 0, 0, 0, 0)),
                  row, pl.BlockSpec((1, D), lambda i, j: (0, 0))],
        out_specs=row,
        out_shape=jax.ShapeDtypeStruct((T, D), F32),
        scratch_shapes=[pltpu.VMEM((n_chunks, D, LANES), F32),
                        pltpu.VMEM((n_chunks, te, LANES), F32), pltpu.VMEM((n_chunks, te, LANES), F32),
                        pltpu.VMEM((n_chunks, te, LANES), BF16), pltpu.VMEM((n_chunks, te, LANES), BF16)],
        compiler_params=_params("parallel", "arbitrary"),
        name="peer_dense",
    )(xn, u, vt, z, h2, g_final)


def _layer(h, mem2, tabs, g_attn, w_in, g_q_a, w_uq, g_kv_a, w_ukv, g_mla_o, g_moba_o, w_out, g_cross,
           g_mem, w_cq, w_ck, w_cv, w_co, g_ffn, w_pq, sub_keys1, sub_keys2, expert_u, expert_v,
           g_final, *, batch, seq, mem_len):
    D = h.shape[1]
    row = lambda g: g.reshape(1, -1)
    n_lat = MLA_Q_RANK + MLA_KV_RANK + MLA_ROPE
    w_in_p = jnp.concatenate([w_in[:, :n_lat], jnp.zeros((D, LANES - MLA_ROPE), w_in.dtype),
                              w_in[:, n_lat:]], axis=1).astype(BF16)
    w_uq_p = jnp.pad(w_uq.reshape(MLA_Q_RANK, MLA_HEADS, MLA_NOPE + MLA_ROPE),
                     ((0, 0), (0, 0), (0, MLA_QK_PAD - MLA_NOPE - MLA_ROPE))
                     ).reshape(MLA_Q_RANK, MLA_HEADS * MLA_QK_PAD).astype(BF16)
    w_ukv3 = w_ukv.reshape(MLA_KV_RANK, MLA_HEADS, MLA_NOPE + MLA_V)
    w_uk = w_ukv3[:, :, :MLA_NOPE].reshape(MLA_KV_RANK, MLA_HEADS * MLA_NOPE).astype(BF16)
    w_uv = w_ukv3[:, :, MLA_NOPE:].reshape(MLA_KV_RANK, MLA_HEADS * MLA_V).astype(BF16)

    ql, kl, vl, qm, km, vm, sel = _attn_prep(h, row(g_attn), w_in_p, row(g_q_a), w_uq_p, row(g_kv_a),
                                             w_uk, w_uv, tabs, seq=seq)
    o_mla = _mla_attention(ql, kl, vl, batch=batch, seq=seq)
    o_moba = _moba_attention(qm, km, vm, sel, batch=batch, seq=seq)
    h1 = _out_proj(h, o_mla, o_moba, row(g_mla_o), row(g_moba_o), w_out.astype(BF16))

    w_ckv = jnp.concatenate([w_ck, w_cv], axis=1).astype(BF16)
    kv = _rms_matmul(mem2, row(g_mem), w_ckv, BF16, tm=min(mem2.shape[0], 512), tn=1024)
    h2 = _cross_attention(h1, row(g_cross), w_cq.astype(BF16), kv, w_co.astype(BF16), seq=seq, mem_len=mem_len)

    xn, z = _peer_select(h2, row(g_ffn), w_pq.astype(BF16), sub_keys1, sub_keys2)
    return _peer_dense(xn, expert_u.astype(BF16), expert_v.T.astype(BF16), z, h2, row(g_final))


def kernel(x, mem, positions, g_attn, w_in, g_q_a, w_uq, g_kv_a, w_ukv, g_mla_o, g_moba_o, w_out, g_cross,
           g_mem, w_cq, w_ck, w_cv, w_co, g_ffn, w_pq, sub_keys1, sub_keys2, expert_u, expert_v, g_final):
    B, S, D = x.shape
    M = mem.shape[1]
    depth = w_in.shape[0]
    assert depth == 1, "the final norm is fused into the layer's last kernel"
    tabs = _rope_tables(positions.reshape(B * S, 1))
    out = _layer(x.reshape(B * S, D), mem.reshape(B * M, D), tabs, g_attn[0], w_in[0], g_q_a[0], w_uq[0],
                 g_kv_a[0], w_ukv[0], g_mla_o[0], g_moba_o[0], w_out[0], g_cross[0], g_mem[0], w_cq[0],
                 w_ck[0], w_cv[0], w_co[0], g_ffn[0], w_pq[0], sub_keys1[0], sub_keys2[0], expert_u[0],
                 expert_v[0], g_final, batch=B, seq=S, mem_len=M)
    return out.reshape(B, S, D)
```

```python
import functools
import math

import jax
import jax.numpy as jnp
from jax import lax
from jax.experimental import pallas as pl
from jax.experimental.pallas import tpu as pltpu

F32 = jnp.float32
BF16 = jnp.bfloat16

EPS = 1e-6
ROPE_THETA = 10000.0
MLA_HEADS = 8
MLA_Q_RANK = 512
MLA_KV_RANK = 256
MLA_NOPE = 128
MLA_ROPE = 64
MLA_V = 128
MLA_QK_PAD = 256
MOBA_HEADS = 8
MOBA_HEAD_DIM = 128
MOBA_BLOCK = 256
MOBA_TOPK = 3
CROSS_HEADS = 4
PEER_HEADS = 8
PEER_NKEYS = 128
PEER_DKEY = 256
PEER_TOPK = 16

LANES = 128
VMEM_LIMIT = 56 * 1024 * 1024

NEG_INF = float("-inf")
NT_DIMS = (((1,), (1,)), ((), ()))


def _rms(x, g):
    ms = jnp.mean(x * x, axis=-1, keepdims=True)
    return x * lax.rsqrt(ms + EPS) * g


def _dot(a, b):
    return jnp.dot(a, b, preferred_element_type=F32)


def _dot_nt(a, b, precision=None):
    return lax.dot_general(a, b, NT_DIMS, precision=precision, preferred_element_type=F32)


def _const_spec(shape):
    nd = len(shape)
    return pl.BlockSpec(shape, lambda *_: (0,) * nd, pipeline_mode=pl.Buffered(1))


def _params(*sem):
    return pltpu.CompilerParams(dimension_semantics=sem, vmem_limit_bytes=VMEM_LIMIT)


def _rope_table_kernel(pos_ref, cm_ref, sm_ref, cl_ref, sl_ref):
    pos = pos_ref[...].astype(F32)
    lane = lax.broadcasted_iota(jnp.int32, (1, LANES), 1)
    half = MOBA_HEAD_DIM // 2
    inv = jnp.exp((lane % half).astype(F32) * (-2.0 * math.log(ROPE_THETA) / MOBA_HEAD_DIM))
    ang = pos * inv
    cm_ref[...] = jnp.cos(ang)
    sn = jnp.sin(ang)
    sm_ref[...] = jnp.where(lane < half, -sn, sn)
    half = MLA_ROPE // 2
    inv = jnp.exp((lane % half).astype(F32) * (-2.0 * math.log(ROPE_THETA) / MLA_ROPE))
    ang = pos * inv
    valid = lane < MLA_ROPE
    cl_ref[...] = jnp.where(valid, jnp.cos(ang), 0.0)
    sn = jnp.sin(ang)
    sl_ref[...] = jnp.where(lane < half, -sn, jnp.where(valid, sn, 0.0))


def _rope_tables(pos_col):
    T = pos_col.shape[0]
    tm = 512
    out = jax.ShapeDtypeStruct((T, LANES), F32)
    spec = pl.BlockSpec((tm, LANES), lambda i: (i, 0))
    return pl.pallas_call(
        _rope_table_kernel,
        grid=(T // tm,),
        in_specs=[pl.BlockSpec((tm, 1), lambda i: (i, 0))],
        out_specs=[spec] * 4,
        out_shape=[out] * 4,
        compiler_params=_params("parallel"),
        name="rope_tables",
    )(pos_col)


def _rope_moba(x, cos, sin):
    return x * cos + pltpu.roll(x, MOBA_HEAD_DIM // 2, 1) * sin


def _rope_mla(x, cos, sin, lane):
    half = MLA_ROPE // 2
    partner = jnp.where(lane < half, pltpu.roll(x, LANES - half, 1), pltpu.roll(x, half, 1))
    return x * cos + partner * sin


C_COLS = MLA_Q_RANK + MLA_KV_RANK + LANES
MOBA_W = MOBA_HEADS * MOBA_HEAD_DIM


def _attn_prep_kernel(x_ref, g_ref, w_in_ref, gq_ref, wuq_ref, gkv_ref, wuk_ref, wuv_ref,
                      cm_ref, sm_ref, cl_ref, sl_ref,
                      ql_ref, kl_ref, vl_ref, qm_ref, km_ref, vm_ref, sel_ref,
                      kbar_ref, qmf_ref, kmf_ref, *, tm, tiles_per_seq, n_blocks):
    i = pl.program_id(0)

    @pl.when(i == 0)
    def _():
        kbar_ref[...] = jnp.zeros_like(kbar_ref)

    xn = _rms(x_ref[...], g_ref[...]).astype(BF16)
    lane = lax.broadcasted_iota(jnp.int32, (1, LANES), 1)
    cl, sl = cl_ref[...], sl_ref[...]
    cm, sm = cm_ref[...], sm_ref[...]

    c = _dot(xn, w_in_ref[:, 0:C_COLS])
    cq = _rms(c[:, 0:MLA_Q_RANK], gq_ref[...]).astype(BF16)
    q = _dot(cq, wuq_ref[...])
    ckv = _rms(c[:, MLA_Q_RANK:MLA_Q_RANK + MLA_KV_RANK], gkv_ref[...]).astype(BF16)
    kn = _dot(ckv, wuk_ref[...])
    vl_ref[...] = _dot(ckv, wuv_ref[...]).astype(BF16)
    kr = _rope_mla(c[:, MLA_Q_RANK + MLA_KV_RANK:C_COLS], cl, sl, lane).astype(BF16)
    for h in range(MLA_HEADS):
        o = h * MLA_QK_PAD
        ql_ref[:, o:o + LANES] = q[:, o:o + LANES].astype(BF16)
        ql_ref[:, o + LANES:o + 2 * LANES] = _rope_mla(q[:, o + LANES:o + 2 * LANES], cl, sl, lane).astype(BF16)
        kl_ref[:, o:o + LANES] = kn[:, h * LANES:(h + 1) * LANES].astype(BF16)
        kl_ref[:, o + LANES:o + 2 * LANES] = kr

    o0 = C_COLS
    qm = _dot(xn, w_in_ref[:, o0:o0 + MOBA_W])
    km = _dot(xn, w_in_ref[:, o0 + MOBA_W:o0 + 2 * MOBA_W])
    vm_ref[...] = _dot(xn, w_in_ref[:, o0 + 2 * MOBA_W:o0 + 3 * MOBA_W]).astype(BF16)
    for h in range(MOBA_HEADS):
        sl_h = slice(h * LANES, (h + 1) * LANES)
        qr = _rope_moba(qm[:, sl_h], cm, sm)
        kr_h = _rope_moba(km[:, sl_h], cm, sm)
        qmf_ref[:, sl_h] = qr
        kmf_ref[:, sl_h] = kr_h
        qm_ref[:, sl_h] = qr.astype(BF16)
        km_ref[:, sl_h] = kr_h.astype(BF16)

    blk0 = (i % tiles_per_seq) * (tm // MOBA_BLOCK)
    for s in range(tm // MOBA_BLOCK):
        kbar_ref[pl.ds(blk0 + s, 1), :] = jnp.mean(
            kmf_ref[s * MOBA_BLOCK:(s + 1) * MOBA_BLOCK, :], axis=0, keepdims=True)
    tok = lax.broadcasted_iota(jnp.int32, (n_blocks, tm), 1)
    own = blk0 + tok // MOBA_BLOCK
    blk = lax.broadcasted_iota(jnp.int32, (n_blocks, tm), 0)
    past = blk < own
    rows = []
    for h in range(MOBA_HEADS):
        sl_h = slice(h * LANES, (h + 1) * LANES)
        gate = _dot_nt(kbar_ref[:, sl_h], qmf_ref[:, sl_h], precision=lax.Precision.HIGHEST)
        gate = jnp.where(past, gate, NEG_INF)
        ahead = jnp.zeros((n_blocks, tm), jnp.int32)
        for m in range(n_blocks):
            gm = gate[m:m + 1, :]
            beats = (gm > gate) | ((gm == gate) & (m < blk))
            ahead = ahead + beats.astype(jnp.int32)
        rows.append(jnp.where(past & (ahead < MOBA_TOPK), 1.0, 0.0).astype(F32))
    pad = jnp.zeros((LANES - MOBA_HEADS * n_blocks, tm), F32)
    sel_ref[...] = jnp.concatenate(rows + [pad], axis=0).T


def _attn_prep(x2, g_attn, w_in_p, g_q, w_uq_p, g_kv, w_uk, w_uv, tabs, *, seq):
    T, D = x2.shape
    tm = 512
    n_blocks = seq // MOBA_BLOCK
    assert seq % tm == 0 and tm % MOBA_BLOCK == 0 and MOBA_HEADS * n_blocks <= LANES
    cm, sm, cl, sl = tabs
    row = lambda w: pl.BlockSpec((tm, w), lambda i: (i, 0))
    kern = functools.partial(_attn_prep_kernel, tm=tm, tiles_per_seq=seq // tm, n_blocks=n_blocks)
    outs = [(MLA_HEADS * MLA_QK_PAD, BF16), (MLA_HEADS * MLA_QK_PAD, BF16), (MLA_HEADS * MLA_V, BF16),
            (MOBA_W, BF16), (MOBA_W, BF16), (MOBA_W, BF16), (LANES, F32)]
    return pl.pallas_call(
        kern,
        grid=(T // tm,),
        in_specs=[row(D), _const_spec(g_attn.shape), _const_spec(w_in_p.shape), _const_spec(g_q.shape),
                  _const_spec(w_uq_p.shape), _const_spec(g_kv.shape), _const_spec(w_uk.shape),
                  _const_spec(w_uv.shape), row(LANES), row(LANES), row(LANES), row(LANES)],
        out_specs=[row(w) for w, _ in outs],
        out_shape=[jax.ShapeDtypeStruct((T, w), dt) for w, dt in outs],
        scratch_shapes=[pltpu.VMEM((n_blocks, MOBA_W), F32), pltpu.VMEM((tm, MOBA_W), F32),
                        pltpu.VMEM((tm, MOBA_W), F32)],
        compiler_params=_params("arbitrary"),
        name="attn_prep",
    )(x2, g_attn, w_in_p, g_q, w_uq_p, g_kv, w_uk, w_uv, cm, sm, cl, sl)


ATTN_TQ = 256


def _softmax_av(s, v):
    m = jnp.max(s, axis=-1, keepdims=True)
    e = jnp.exp(s - m)
    l = jnp.sum(e, axis=-1, keepdims=True)
    return _dot(e.astype(BF16), v) / l


def _causal_bias(tq):
    r = lax.broadcasted_iota(jnp.int32, (tq, tq), 0)
    c = lax.broadcasted_iota(jnp.int32, (tq, tq), 1)
    return jnp.where(c <= r, 0.0, NEG_INF).astype(F32)


def _mla_attn_kernel(q_ref, k_ref, v_ref, o_ref, *, seq, scale):
    tq = ATTN_TQ
    diag = _causal_bias(tq)
    for i in range(seq // tq):
        n = (i + 1) * tq
        s = _dot_nt(q_ref[i * tq:n, :], k_ref[0:n, :]) * scale
        if i:
            bias = jnp.concatenate([jnp.zeros((tq, i * tq), F32), diag], axis=1)
        else:
            bias = diag
        o_ref[i * tq:n, :] = _softmax_av(s + bias, v_ref[0:n, :])


def _mla_attention(ql, kl, vl, *, batch, seq):
    T = ql.shape[0]
    scale = (MLA_NOPE + MLA_ROPE) ** -0.5
    return pl.pallas_call(
        functools.partial(_mla_attn_kernel, seq=seq, scale=scale),
        grid=(batch, MLA_HEADS),
        in_specs=[pl.BlockSpec((seq, MLA_QK_PAD), lambda b, h: (b, h)),
                  pl.BlockSpec((seq, MLA_QK_PAD), lambda b, h: (b, h)),
                  pl.BlockSpec((seq, MLA_V), lambda b, h: (b, h))],
        out_specs=pl.BlockSpec((seq, MLA_V), lambda b, h: (b, h)),
        out_shape=jax.ShapeDtypeStruct((T, MLA_HEADS * MLA_V), F32),
        compiler_params=_params("parallel", "parallel"),
        name="mla_attention",
    )(ql, kl, vl)


def _moba_attn_kernel(q_ref, k_ref, v_ref, sel_ref, o_ref, *, seq, scale, n_blocks):
    tq = ATTN_TQ
    h = pl.program_id(1)
    lane = lax.broadcasted_iota(jnp.int32, (1, LANES), 1)
    diag = _causal_bias(tq)
    for i in range(seq // tq):
        n = (i + 1) * tq
        s = _dot_nt(q_ref[i * tq:n, :], k_ref[0:n, :]) * scale
        sel = sel_ref[i * tq:n, :]
        pieces = []
        for b in range(i):
            chosen = jnp.sum(jnp.where(lane == h * n_blocks + b, sel, 0.0), axis=-1, keepdims=True)
            pieces.append(jnp.broadcast_to(jnp.where(chosen > 0.0, 0.0, NEG_INF), (tq, tq)))
        bias = jnp.concatenate(pieces + [diag], axis=1) if pieces else diag
        o_ref[i * tq:n, :] = _softmax_av(s + bias, v_ref[0:n, :])


def _moba_attention(qm, km, vm, sel, *, batch, seq):
    T = qm.shape[0]
    assert ATTN_TQ == MOBA_BLOCK
    hd = pl.BlockSpec((seq, MOBA_HEAD_DIM), lambda b, h: (b, h))
    return pl.pallas_call(
        functools.partial(_moba_attn_kernel, seq=seq, scale=MOBA_HEAD_DIM ** -0.5,
                          n_blocks=seq // MOBA_BLOCK),
        grid=(batch, MOBA_HEADS),
        in_specs=[hd, hd, hd, pl.BlockSpec((seq, LANES), lambda b, h: (b, 0))],
        out_specs=hd,
        out_shape=jax.ShapeDtypeStruct((T, MOBA_W), F32),
        compiler_params=_params("parallel", "parallel"),
        name="moba_attention",
    )(qm, km, vm, sel)


def _out_proj_kernel(x_ref, ol_ref, om_ref, gl_ref, gm_ref, w_ref, h_ref, *, wl):
    a = _rms(ol_ref[...], gl_ref[...]).astype(BF16)
    b = _rms(om_ref[...], gm_ref[...]).astype(BF16)
    h_ref[...] = x_ref[...] + (_dot(a, w_ref[0:wl, :]) + _dot(b, w_ref[wl:, :]))


def _out_proj(x2, o_mla, o_moba, g_l, g_m, w_out):
    T, D = x2.shape
    tm = 512
    row = lambda w: pl.BlockSpec((tm, w), lambda i: (i, 0))
    return pl.pallas_call(
        functools.partial(_out_proj_kernel, wl=o_mla.shape[1]),
        grid=(T // tm,),
        in_specs=[row(D), row(o_mla.shape[1]), row(o_moba.shape[1]), _const_spec(g_l.shape),
                  _const_spec(g_m.shape), _const_spec(w_out.shape)],
        out_specs=row(D),
        out_shape=jax.ShapeDtypeStruct((T, D), F32),
        compiler_params=_params("parallel"),
        name="out_proj",
    )(x2, o_mla, o_moba, g_l, g_m, w_out)


def _rms_matmul_kernel(x_ref, g_ref, w_ref, o_ref):
    xn = _rms(x_ref[...], g_ref[...]).astype(BF16)
    o_ref[...] = _dot(xn, w_ref[...]).astype(o_ref.dtype)


def _rms_matmul(x, g, w, out_dtype, *, tm, tn):
    M, K = x.shape
    N = w.shape[1]
    return pl.pallas_call(
        _rms_matmul_kernel,
        grid=(M // tm, N // tn),
        in_specs=[pl.BlockSpec((tm, K), lambda i, j: (i, 0)), pl.BlockSpec((1, K), lambda i, j: (0, 0)),
                  pl.BlockSpec((K, tn), lambda i, j: (0, j))],
        out_specs=pl.BlockSpec((tm, tn), lambda i, j: (i, j)),
        out_shape=jax.ShapeDtypeStruct((M, N), out_dtype),
        compiler_params=_params("parallel", "parallel"),
        name="rms_matmul",
    )(x, g, w)


def _cross_kernel(h_ref, g_ref, wq_ref, kv_ref, wo_ref, o_ref, *, scale):
    h = h_ref[...]
    D = h.shape[1]
    dh = D // CROSS_HEADS
    q = _dot(_rms(h, g_ref[...]).astype(BF16), wq_ref[...]).astype(BF16)
    outs = []
    for hd in range(CROSS_HEADS):
        s = _dot_nt(q[:, hd * dh:(hd + 1) * dh], kv_ref[:, hd * dh:(hd + 1) * dh]) * scale
        outs.append(_softmax_av(s, kv_ref[:, D + hd * dh:D + (hd + 1) * dh]).astype(BF16))
    o_ref[...] = h + _dot(jnp.concatenate(outs, axis=1), wo_ref[...])


def _cross_attention(h1, g_cross, w_cq, kv, w_co, *, seq, mem_len):
    T, D = h1.shape
    tm = 256
    tiles_per_seq = seq // tm
    row = pl.BlockSpec((tm, D), lambda i: (i, 0))
    return pl.pallas_call(
        functools.partial(_cross_kernel, scale=(D // CROSS_HEADS) ** -0.5),
        grid=(T // tm,),
        in_specs=[row, _const_spec(g_cross.shape), _const_spec(w_cq.shape),
                  pl.BlockSpec((mem_len, 2 * D), lambda i: (i // tiles_per_seq, 0)),
                  _const_spec(w_co.shape)],
        out_specs=row,
        out_shape=jax.ShapeDtypeStruct((T, D), F32),
        compiler_params=_params("parallel"),
        name="cross_attention",
    )(h1, g_cross, w_cq, kv, w_co)


def _top16_cols(s, exact):
    n, tm = s.shape
    idx = lax.broadcasted_iota(jnp.int32, (n, tm), 0).astype(F32) if exact else None
    rank = jnp.full((n, tm), float(PEER_TOPK), F32)
    vals = []
    for r in range(PEER_TOPK):
        m = jnp.max(s, axis=0, keepdims=True)
        hit = s == m
        if exact:
            hit = idx == jnp.min(jnp.where(hit, idx, float(n)), axis=0, keepdims=True)
        rank = jnp.where(hit, float(r), rank)
        s = jnp.where(hit, NEG_INF, s)
        vals.append(m)
    kept = jnp.sum(jnp.where(rank < float(PEER_TOPK), 1.0, 0.0), axis=0, keepdims=True)
    return vals, rank, kept == float(PEER_TOPK)


_PAIR_B = [PEER_TOPK // (a + 1) for a in range(8)]


def _pair_counts(v1, v2, exact):
    tm = v1[0].shape[1]
    v1s = jnp.concatenate(v1, axis=0)
    v2s = jnp.concatenate(v2, axis=0)
    pieces, keys = [], []
    for a, nb in enumerate(_PAIR_B):
        rows = -(-nb // 8) * 8
        r = lax.broadcasted_iota(jnp.int32, (rows, tm), 0)
        pieces.append(jnp.where(r < nb, v2s[0:rows, :] + v1[a], NEG_INF))
        keys.append((a * PEER_TOPK + r).astype(F32))
    r = lax.broadcasted_iota(jnp.int32, (8, tm), 0)
    pieces.append(v1s[8:16, :] + v2[0])
    keys.append(((r + 8) * PEER_TOPK).astype(F32))
    cand = jnp.concatenate(pieces, axis=0)
    key = jnp.concatenate(keys, axis=0) if exact else None
    work = cand
    for _ in range(PEER_TOPK):
        m = jnp.max(work, axis=0, keepdims=True)
        hit = work == m
        if exact:
            hit = key == jnp.min(jnp.where(hit, key, float(PEER_TOPK * PEER_TOPK)), axis=0, keepdims=True)
        work = jnp.where(hit, NEG_INF, work)
    chosen = work != cand
    top = v1[0] + v2[0]
    z = jnp.sum(jnp.where(chosen, jnp.exp(cand - top), 0.0), axis=0, keepdims=True)
    picked = jnp.where(chosen, 1.0, 0.0)
    counts, o = [], 0
    for nb in _PAIR_B:
        rows = -(-nb // 8) * 8
        counts.append(jnp.sum(picked[o:o + rows, :], axis=0, keepdims=True))
        o += rows
    for a in range(8):
        counts.append(picked[o + a:o + a + 1, :])
    total = jnp.sum(picked, axis=0, keepdims=True)
    return counts, z, total == float(PEER_TOPK)


Z_E1, Z_CNT, Z_RANK2, Z_E2 = range(4)
Z_GROUPS = PEER_NKEYS // 8


def _peer_retrieve(q_ref, k1_ref, k2_ref, z_ref, exact):
    half = PEER_DKEY // 2
    hi = lax.Precision.HIGHEST
    unresolved = jnp.zeros((1, LANES), F32)
    for c in range(q_ref.shape[0] // LANES):
        ts = slice(c * LANES, (c + 1) * LANES)
        for hd in range(PEER_HEADS):
            o = hd * PEER_DKEY
            s1 = _dot_nt(k1_ref[...], q_ref[ts, o:o + half], precision=hi)
            s2 = _dot_nt(k2_ref[...], q_ref[ts, o + half:o + 2 * half], precision=hi)
            v1, rank1, ok1 = _top16_cols(s1, exact)
            v2, rank2, ok2 = _top16_cols(s2, exact)
            counts, z, okp = _pair_counts(v1, v2, exact)
            unresolved = jnp.where(ok1 & ok2 & okp, unresolved, 1.0)
            cnt = jnp.zeros(s1.shape, F32)
            for a in range(PEER_TOPK):
                cnt = jnp.where(rank1 == float(a), counts[a], cnt)
            planes = {Z_E1: jnp.exp(s1 - v1[0]), Z_CNT: cnt, Z_RANK2: rank2, Z_E2: jnp.exp(s2 - v2[0]) / z}
            for p, val in planes.items():
                z_ref[c, hd, :, p] = val.reshape(Z_GROUPS, 8, LANES)
    return unresolved


def _peer_select_kernel(h_ref, g_ref, wq_ref, k1_ref, k2_ref, xn_ref, z_ref, q_ref):
    xn = _rms(h_ref[...], g_ref[...]).astype(BF16)
    xn_ref[...] = xn
    q_ref[...] = _dot(xn, wq_ref[...])
    unresolved = _peer_retrieve(q_ref, k1_ref, k2_ref, z_ref, exact=False)

    @pl.when(jnp.max(unresolved) > 0.0)
    def _():
        _peer_retrieve(q_ref, k1_ref, k2_ref, z_ref, exact=True)


def _peer_select(h2, g_ffn, w_pq, k1, k2):
    T, D = h2.shape
    tm = 256
    row = pl.BlockSpec((tm, D), lambda i: (i, 0))
    z_block = (tm // LANES, PEER_HEADS, Z_GROUPS, 4, 8, LANES)
    return pl.pallas_call(
        _peer_select_kernel,
        grid=(T // tm,),
        in_specs=[row, _const_spec(g_ffn.shape), _const_spec(w_pq.shape), _const_spec(k1.shape),
                  _const_spec(k2.shape)],
        out_specs=[row, pl.BlockSpec(z_block, lambda i: (i, 0, 0, 0, 0, 0))],
        out_shape=[jax.ShapeDtypeStruct((T, D), BF16),
                   jax.ShapeDtypeStruct((T // LANES,) + z_block[1:], F32)],
        scratch_shapes=[pltpu.VMEM((tm, w_pq.shape[1]), F32)],
        compiler_params=_params("parallel"),
        name="peer_select",
    )(h2, g_ffn, w_pq, k1, k2)


GATE_ROWS = 32
MXU_COLS = 256


def _peer_gate_block(at_ref, wt_ref, z_ref, group, sub0, c, sg, *, te):
    n_i1 = te // PEER_NKEYS
    groups_per_block = GATE_ROWS // 8
    gs = slice(sg * groups_per_block, (sg + 1) * groups_per_block)
    gate = [None] * n_i1
    for hd in range(PEER_HEADS):
        rank2 = z_ref[c, hd, gs, Z_RANK2]
        e2 = z_ref[c, hd, gs, Z_E2]
        for r in range(n_i1):
            rs = slice(sub0 + r, sub0 + r + 1)
            e1 = z_ref[c, hd, group, Z_E1, rs, :]
            cnt = z_ref[c, hd, group, Z_CNT, rs, :]
            term = e1 * jnp.where(rank2 < cnt, e2, 0.0)
            gate[r] = term if gate[r] is None else gate[r] + term
    for r in range(n_i1):
        es = slice(r * PEER_NKEYS + sg * GATE_ROWS, r * PEER_NKEYS + (sg + 1) * GATE_ROWS)
        a = at_ref[c, es, :]
        act = 0.5 * a * (1.0 + lax.erf(a * math.sqrt(0.5)))
        wt_ref[c, es, :] = (gate[r].reshape(GATE_ROWS, LANES) * act).astype(BF16)


def _peer_dense_kernel(xn_ref, u_ref, vt_ref, z_ref, h_ref, gf_ref,
                       o_ref, acc_ref, at0_ref, at1_ref, wt0_ref, wt1_ref, *, te, tm, n_tiles):
    j = pl.program_id(1)

    @pl.when(j == 0)
    def _():
        acc_ref[...] = jnp.zeros_like(acc_ref)
        at1_ref[...] = jnp.zeros_like(at1_ref)
        wt0_ref[...] = jnp.zeros_like(wt0_ref)

    rows_per_step = te // PEER_NKEYS
    assert 2 * rows_per_step == 8 and n_tiles % 2 == 0

    n_chunks = tm // LANES
    assert n_chunks == 4 and 2 * LANES == MXU_COLS
    n_sg = PEER_NKEYS // GATE_ROWS
    d_piece = acc_ref.shape[1] // (n_chunks * n_sg // 2)

    def step(at_w, at_r, wt_w, wt_r, tile_b, sub0):
        for c in range(n_chunks):
            for sg in range(n_sg):
                k = c * n_sg + sg
                p, dr = k % 2, slice((k // 2) * d_piece, (k // 2 + 1) * d_piece)
                w = jnp.concatenate([wt_r[2 * p], wt_r[2 * p + 1]], axis=1)
                res = _dot(vt_ref[dr, :], w)
                acc_ref[2 * p, dr, :] += res[:, :LANES]
                acc_ref[2 * p + 1, dr, :] += res[:, LANES:]
                _peer_gate_block(at_r, wt_w, z_ref, tile_b // 2, sub0, c, sg, te=te)
            c0, er = 2 * (c // 2), slice((c % 2) * (te // 2), (c % 2 + 1) * (te // 2))
            res = _dot_nt(u_ref[er, :], xn_ref[c0 * LANES:(c0 + 2) * LANES, :])
            at_w[c0, er, :] = res[:, :LANES]
            at_w[c0 + 1, er, :] = res[:, LANES:]

    @pl.when(j % 2 == 0)
    def _():
        step(at0_ref, at1_ref, wt1_ref, wt0_ref, jnp.maximum(j - 1, 1), rows_per_step)

    @pl.when(j % 2 == 1)
    def _():
        step(at1_ref, at0_ref, wt0_ref, wt1_ref, jnp.minimum(j - 1, n_tiles - 2), 0)

    @pl.when(j == pl.num_programs(1) - 1)
    def _():
        for c in range(n_chunks):
            rows = slice(c * LANES, (c + 1) * LANES)
            o_ref[rows, :] = _rms(h_ref[rows, :] + acc_ref[c].T, gf_ref[...])


PEER_TE = 512


def _peer_dense(xn, u, vt, z, h2, g_final):
    T, D = h2.shape
    tm, te = 512, PEER_TE
    n_tiles = vt.shape[0]
    n_chunks = tm // LANES
    clamp = lambda t: jnp.minimum(jnp.maximum(t, 0), n_tiles - 1)
    row = pl.BlockSpec((tm, D), lambda i, j: (i, 0))
    return pl.pallas_call(
        functools.partial(_peer_dense_kernel, te=te, tm=tm, n_tiles=n_tiles),
        grid=(T // tm, n_tiles + 2),
        in_specs=[row, pl.BlockSpec((te, D), lambda i, j: (clamp(j), 0)),
                  pl.BlockSpec((None, D, te), lambda i, j: (clamp(j - 2), 0, 0)),
                  pl.BlockSpec((n_chunks,) + z.shape[1:], lambda i, j: (i, 0, 0, 0, 0, 0)),
                  row, pl.BlockSpec((1, D), lambda i, j: (0, 0))],
        out_specs=row,
        out_shape=jax.ShapeDtypeStruct((T, D), F32),
        scratch_shapes=[pltpu.VMEM((n_chunks, D, LANES), F32),
                        pltpu.VMEM((n_chunks, te, LANES), F32), pltpu.VMEM((n_chunks, te, LANES), F32),
                        pltpu.VMEM((n_chunks, te, LANES), BF16), pltpu.VMEM((n_chunks, te, LANES), BF16)],
        compiler_params=_params("parallel", "arbitrary"),
        name="peer_dense",
    )(xn, u, vt, z, h2, g_final)


def _layer(h, mem2, tabs, g_attn, w_in, g_q_a, w_uq, g_kv_a, w_ukv, g_mla_o, g_moba_o, w_out, g_cross,
           g_mem, w_cq, w_ck, w_cv, w_co, g_ffn, w_pq, sub_keys1, sub_keys2, expert_u, expert_v,
           g_final, *, batch, seq, mem_len):
    D = h.shape[1]
    row = lambda g: g.reshape(1, -1)
    n_lat = MLA_Q_RANK + MLA_KV_RANK + MLA_ROPE
    w_in_p = jnp.concatenate([w_in[:, :n_lat], jnp.zeros((D, LANES - MLA_ROPE), w_in.dtype),
                              w_in[:, n_lat:]], axis=1).astype(BF16)
    w_uq_p = jnp.pad(w_uq.reshape(MLA_Q_RANK, MLA_HEADS, MLA_NOPE + MLA_ROPE),
                     ((0, 0), (0, 0), (0, MLA_QK_PAD - MLA_NOPE - MLA_ROPE))
                     ).reshape(MLA_Q_RANK, MLA_HEADS * MLA_QK_PAD).astype(BF16)
    w_ukv3 = w_ukv.reshape(MLA_KV_RANK, MLA_HEADS, MLA_NOPE + MLA_V)
    w_uk = w_ukv3[:, :, :MLA_NOPE].reshape(MLA_KV_RANK, MLA_HEADS * MLA_NOPE).astype(BF16)
    w_uv = w_ukv3[:, :, MLA_NOPE:].reshape(MLA_KV_RANK, MLA_HEADS * MLA_V).astype(BF16)

    ql, kl, vl, qm, km, vm, sel = _attn_prep(h, row(g_attn), w_in_p, row(g_q_a), w_uq_p, row(g_kv_a),
                                             w_uk, w_uv, tabs, seq=seq)
    o_mla = _mla_attention(ql, kl, vl, batch=batch, seq=seq)
    o_moba = _moba_attention(qm, km, vm, sel, batch=batch, seq=seq)
    h1 = _out_proj(h, o_mla, o_moba, row(g_mla_o), row(g_moba_o), w_out.astype(BF16))

    w_ckv = jnp.concatenate([w_ck, w_cv], axis=1).astype(BF16)
    kv = _rms_matmul(mem2, row(g_mem), w_ckv, BF16, tm=min(mem2.shape[0], 512), tn=1024)
    h2 = _cross_attention(h1, row(g_cross), w_cq.astype(BF16), kv, w_co.astype(BF16), seq=seq, mem_len=mem_len)

    xn, z = _peer_select(h2, row(g_ffn), w_pq.astype(BF16), sub_keys1, sub_keys2)
    vt = expert_v.reshape(-1, PEER_TE, D).transpose(0, 2, 1).astype(BF16)
    return _peer_dense(xn, expert_u.astype(BF16), vt, z, h2, row(g_final))


def kernel(x, mem, positions, g_attn, w_in, g_q_a, w_uq, g_kv_a, w_ukv, g_mla_o, g_moba_o, w_out, g_cross,
           g_mem, w_cq, w_ck, w_cv, w_co, g_ffn, w_pq, sub_keys1, sub_keys2, expert_u, expert_v, g_final):
    B, S, D = x.shape
    M = mem.shape[1]
    depth = w_in.shape[0]
    assert depth == 1, "the final norm is fused into the layer's last kernel"
    tabs = _rope_tables(positions.reshape(B * S, 1))
    out = _layer(x.reshape(B * S, D), mem.reshape(B * M, D), tabs, g_attn[0], w_in[0], g_q_a[0], w_uq[0],
                 g_kv_a[0], w_ukv[0], g_mla_o[0], g_moba_o[0], w_out[0], g_cross[0], g_mem[0], w_cq[0],
                 w_ck[0], w_cv[0], w_co[0], g_ffn[0], w_pq[0], sub_keys1[0], sub_keys2[0], expert_u[0],
                 expert_v[0], g_final, batch=B, seq=S, mem_len=M)
    return out.reshape(B, S, D)
```

```python
import functools
import math

import jax
import jax.numpy as jnp
from jax import lax
from jax.experimental import pallas as pl
from jax.experimental.pallas import tpu as pltpu

F32 = jnp.float32
BF16 = jnp.bfloat16

EPS = 1e-6
ROPE_THETA = 10000.0
MLA_HEADS = 8
MLA_Q_RANK = 512
MLA_KV_RANK = 256
MLA_NOPE = 128
MLA_ROPE = 64
MLA_V = 128
MLA_QK_PAD = 256
MOBA_HEADS = 8
MOBA_HEAD_DIM = 128
MOBA_BLOCK = 256
MOBA_TOPK = 3
CROSS_HEADS = 4
PEER_HEADS = 8
PEER_NKEYS = 128
PEER_DKEY = 256
PEER_TOPK = 16

LANES = 128
VMEM_LIMIT = 56 * 1024 * 1024

NEG_INF = float("-inf")
NT_DIMS = (((1,), (1,)), ((), ()))


def _rms(x, g):
    ms = jnp.mean(x * x, axis=-1, keepdims=True)
    return x * lax.rsqrt(ms + EPS) * g


def _dot(a, b):
    return jnp.dot(a, b, preferred_element_type=F32)


def _dot_nt(a, b, precision=None):
    return lax.dot_general(a, b, NT_DIMS, precision=precision, preferred_element_type=F32)


def _const_spec(shape):
    nd = len(shape)
    return pl.BlockSpec(shape, lambda *_: (0,) * nd, pipeline_mode=pl.Buffered(1))


def _params(*sem):
    return pltpu.CompilerParams(dimension_semantics=sem, vmem_limit_bytes=VMEM_LIMIT)


def _rope_table_kernel(pos_ref, cm_ref, sm_ref, cl_ref, sl_ref):
    pos = pos_ref[...].astype(F32)
    lane = lax.broadcasted_iota(jnp.int32, (1, LANES), 1)
    half = MOBA_HEAD_DIM // 2
    inv = jnp.exp((lane % half).astype(F32) * (-2.0 * math.log(ROPE_THETA) / MOBA_HEAD_DIM))
    ang = pos * inv
    cm_ref[...] = jnp.cos(ang)
    sn = jnp.sin(ang)
    sm_ref[...] = jnp.where(lane < half, -sn, sn)
    half = MLA_ROPE // 2
    inv = jnp.exp((lane % half).astype(F32) * (-2.0 * math.log(ROPE_THETA) / MLA_ROPE))
    ang = pos * inv
    valid = lane < MLA_ROPE
    cl_ref[...] = jnp.where(valid, jnp.cos(ang), 0.0)
    sn = jnp.sin(ang)
    sl_ref[...] = jnp.where(lane < half, -sn, jnp.where(valid, sn, 0.0))


def _rope_tables(pos_col):
    T = pos_col.shape[0]
    tm = 512
    out = jax.ShapeDtypeStruct((T, LANES), F32)
    spec = pl.BlockSpec((tm, LANES), lambda i: (i, 0))
    return pl.pallas_call(
        _rope_table_kernel,
        grid=(T // tm,),
        in_specs=[pl.BlockSpec((tm, 1), lambda i: (i, 0))],
        out_specs=[spec] * 4,
        out_shape=[out] * 4,
        compiler_params=_params("parallel"),
        name="rope_tables",
    )(pos_col)


def _rope_moba(x, cos, sin):
    return x * cos + pltpu.roll(x, MOBA_HEAD_DIM // 2, 1) * sin


def _rope_mla(x, cos, sin, lane):
    half = MLA_ROPE // 2
    partner = jnp.where(lane < half, pltpu.roll(x, LANES - half, 1), pltpu.roll(x, half, 1))
    return x * cos + partner * sin


C_COLS = MLA_Q_RANK + MLA_KV_RANK + LANES
MOBA_W = MOBA_HEADS * MOBA_HEAD_DIM


def _attn_prep_kernel(x_ref, g_ref, w_in_ref, gq_ref, wuq_ref, gkv_ref, wuk_ref, wuv_ref,
                      cm_ref, sm_ref, cl_ref, sl_ref,
                      ql_ref, kl_ref, vl_ref, qm_ref, km_ref, vm_ref, sel_ref,
                      kbar_ref, qmf_ref, kmf_ref, *, tm, tiles_per_seq, n_blocks):
    i = pl.program_id(0)

    @pl.when(i == 0)
    def _():
        kbar_ref[...] = jnp.zeros_like(kbar_ref)

    xn = _rms(x_ref[...], g_ref[...]).astype(BF16)
    lane = lax.broadcasted_iota(jnp.int32, (1, LANES), 1)
    cl, sl = cl_ref[...], sl_ref[...]
    cm, sm = cm_ref[...], sm_ref[...]

    c = _dot(xn, w_in_ref[:, 0:C_COLS])
    cq = _rms(c[:, 0:MLA_Q_RANK], gq_ref[...]).astype(BF16)
    q = _dot(cq, wuq_ref[...])
    ckv = _rms(c[:, MLA_Q_RANK:MLA_Q_RANK + MLA_KV_RANK], gkv_ref[...]).astype(BF16)
    kn = _dot(ckv, wuk_ref[...])
    vl_ref[...] = _dot(ckv, wuv_ref[...]).astype(BF16)
    kr = _rope_mla(c[:, MLA_Q_RANK + MLA_KV_RANK:C_COLS], cl, sl, lane).astype(BF16)
    for h in range(MLA_HEADS):
        o = h * MLA_QK_PAD
        ql_ref[:, o:o + LANES] = q[:, o:o + LANES].astype(BF16)
        ql_ref[:, o + LANES:o + 2 * LANES] = _rope_mla(q[:, o + LANES:o + 2 * LANES], cl, sl, lane).astype(BF16)
        kl_ref[:, o:o + LANES] = kn[:, h * LANES:(h + 1) * LANES].astype(BF16)
        kl_ref[:, o + LANES:o + 2 * LANES] = kr

    o0 = C_COLS
    qm = _dot(xn, w_in_ref[:, o0:o0 + MOBA_W])
    km = _dot(xn, w_in_ref[:, o0 + MOBA_W:o0 + 2 * MOBA_W])
    vm_ref[...] = _dot(xn, w_in_ref[:, o0 + 2 * MOBA_W:o0 + 3 * MOBA_W]).astype(BF16)
    for h in range(MOBA_HEADS):
        sl_h = slice(h * LANES, (h + 1) * LANES)
        qr = _rope_moba(qm[:, sl_h], cm, sm)
        kr_h = _rope_moba(km[:, sl_h], cm, sm)
        qmf_ref[:, sl_h] = qr
        kmf_ref[:, sl_h] = kr_h
        qm_ref[:, sl_h] = qr.astype(BF16)
        km_ref[:, sl_h] = kr_h.astype(BF16)

    blk0 = (i % tiles_per_seq) * (tm // MOBA_BLOCK)
    for s in range(tm // MOBA_BLOCK):
        kbar_ref[pl.ds(blk0 + s, 1), :] = jnp.mean(
            kmf_ref[s * MOBA_BLOCK:(s + 1) * MOBA_BLOCK, :], axis=0, keepdims=True)
    tok = lax.broadcasted_iota(jnp.int32, (n_blocks, tm), 1)
    own = blk0 + tok // MOBA_BLOCK
    blk = lax.broadcasted_iota(jnp.int32, (n_blocks, tm), 0)
    past = blk < own
    rows = []
    for h in range(MOBA_HEADS):
        sl_h = slice(h * LANES, (h + 1) * LANES)
        gate = _dot_nt(kbar_ref[:, sl_h], qmf_ref[:, sl_h], precision=lax.Precision.HIGHEST)
        gate = jnp.where(past, gate, NEG_INF)
        ahead = jnp.zeros((n_blocks, tm), jnp.int32)
        for m in range(n_blocks):
            gm = gate[m:m + 1, :]
            beats = (gm > gate) | ((gm == gate) & (m < blk))
            ahead = ahead + beats.astype(jnp.int32)
        rows.append(jnp.where(past & (ahead < MOBA_TOPK), 1.0, 0.0).astype(F32))
    pad = jnp.zeros((LANES - MOBA_HEADS * n_blocks, tm), F32)
    sel_ref[...] = jnp.concatenate(rows + [pad], axis=0).T


def _attn_prep(x2, g_attn, w_in_p, g_q, w_uq_p, g_kv, w_uk, w_uv, tabs, *, seq):
    T, D = x2.shape
    tm = 512
    n_blocks = seq // MOBA_BLOCK
    assert seq % tm == 0 and tm % MOBA_BLOCK == 0 and MOBA_HEADS * n_blocks <= LANES
    cm, sm, cl, sl = tabs
    row = lambda w: pl.BlockSpec((tm, w), lambda i: (i, 0))
    kern = functools.partial(_attn_prep_kernel, tm=tm, tiles_per_seq=seq // tm, n_blocks=n_blocks)
    outs = [(MLA_HEADS * MLA_QK_PAD, BF16), (MLA_HEADS * MLA_QK_PAD, BF16), (MLA_HEADS * MLA_V, BF16),
            (MOBA_W, BF16), (MOBA_W, BF16), (MOBA_W, BF16), (LANES, F32)]
    return pl.pallas_call(
        kern,
        grid=(T // tm,),
        in_specs=[row(D), _const_spec(g_attn.shape), _const_spec(w_in_p.shape), _const_spec(g_q.shape),
                  _const_spec(w_uq_p.shape), _const_spec(g_kv.shape), _const_spec(w_uk.shape),
                  _const_spec(w_uv.shape), row(LANES), row(LANES), row(LANES), row(LANES)],
        out_specs=[row(w) for w, _ in outs],
        out_shape=[jax.ShapeDtypeStruct((T, w), dt) for w, dt in outs],
        scratch_shapes=[pltpu.VMEM((n_blocks, MOBA_W), F32), pltpu.VMEM((tm, MOBA_W), F32),
                        pltpu.VMEM((tm, MOBA_W), F32)],
        compiler_params=_params("arbitrary"),
        name="attn_prep",
    )(x2, g_attn, w_in_p, g_q, w_uq_p, g_kv, w_uk, w_uv, cm, sm, cl, sl)


ATTN_TQ = 256


def _softmax_av(s, v):
    m = jnp.max(s, axis=-1, keepdims=True)
    e = jnp.exp(s - m)
    l = jnp.sum(e, axis=-1, keepdims=True)
    return _dot(e.astype(BF16), v) / l


def _causal_bias(tq):
    r = lax.broadcasted_iota(jnp.int32, (tq, tq), 0)
    c = lax.broadcasted_iota(jnp.int32, (tq, tq), 1)
    return jnp.where(c <= r, 0.0, NEG_INF).astype(F32)


def _pipelined_tiles(scores, attend, o_ref, n_tiles, tq):
    s_next = scores(0)
    for i in range(n_tiles):
        s = s_next
        if i + 1 < n_tiles:
            s_next = scores(i + 1)
        o_ref[i * tq:(i + 1) * tq, :] = attend(s, (i + 1) * tq)


def _mla_attn_kernel(q_ref, k_ref, v_ref, o_ref, *, seq, scale):
    tq = ATTN_TQ
    diag = _causal_bias(tq)

    def scores(i):
        n = (i + 1) * tq
        s = _dot_nt(q_ref[i * tq:n, :], k_ref[0:n, :]) * scale
        return s + jnp.concatenate([jnp.zeros((tq, i * tq), F32), diag], axis=1) if i else s + diag

    _pipelined_tiles(scores, lambda s, n: _softmax_av(s, v_ref[0:n, :]), o_ref, seq // tq, tq)


def _mla_attention(ql, kl, vl, *, batch, seq):
    T = ql.shape[0]
    scale = (MLA_NOPE + MLA_ROPE) ** -0.5
    return pl.pallas_call(
        functools.partial(_mla_attn_kernel, seq=seq, scale=scale),
        grid=(batch, MLA_HEADS),
        in_specs=[pl.BlockSpec((seq, MLA_QK_PAD), lambda b, h: (b, h)),
                  pl.BlockSpec((seq, MLA_QK_PAD), lambda b, h: (b, h)),
                  pl.BlockSpec((seq, MLA_V), lambda b, h: (b, h))],
        out_specs=pl.BlockSpec((seq, MLA_V), lambda b, h: (b, h)),
        out_shape=jax.ShapeDtypeStruct((T, MLA_HEADS * MLA_V), F32),
        compiler_params=_params("parallel", "parallel"),
        name="mla_attention",
    )(ql, kl, vl)


def _moba_attn_kernel(q_ref, k_ref, v_ref, sel_ref, o_ref, *, seq, scale, n_blocks):
    tq = ATTN_TQ
    h = pl.program_id(1)
    lane = lax.broadcasted_iota(jnp.int32, (1, LANES), 1)
    diag = _causal_bias(tq)

    def scores(i):
        n = (i + 1) * tq
        s = _dot_nt(q_ref[i * tq:n, :], k_ref[0:n, :]) * scale
        sel = sel_ref[i * tq:n, :]
        pieces = []
        for b in range(i):
            chosen = jnp.sum(jnp.where(lane == h * n_blocks + b, sel, 0.0), axis=-1, keepdims=True)
            pieces.append(jnp.broadcast_to(jnp.where(chosen > 0.0, 0.0, NEG_INF), (tq, tq)))
        return s + (jnp.concatenate(pieces + [diag], axis=1) if pieces else diag)

    _pipelined_tiles(scores, lambda s, n: _softmax_av(s, v_ref[0:n, :]), o_ref, seq // tq, tq)


def _moba_attention(qm, km, vm, sel, *, batch, seq):
    T = qm.shape[0]
    assert ATTN_TQ == MOBA_BLOCK
    hd = pl.BlockSpec((seq, MOBA_HEAD_DIM), lambda b, h: (b, h))
    return pl.pallas_call(
        functools.partial(_moba_attn_kernel, seq=seq, scale=MOBA_HEAD_DIM ** -0.5,
                          n_blocks=seq // MOBA_BLOCK),
        grid=(batch, MOBA_HEADS),
        in_specs=[hd, hd, hd, pl.BlockSpec((seq, LANES), lambda b, h: (b, 0))],
        out_specs=hd,
        out_shape=jax.ShapeDtypeStruct((T, MOBA_W), F32),
        compiler_params=_params("parallel", "parallel"),
        name="moba_attention",
    )(qm, km, vm, sel)


def _out_proj_kernel(x_ref, ol_ref, om_ref, gl_ref, gm_ref, w_ref, h_ref, *, wl):
    a = _rms(ol_ref[...], gl_ref[...]).astype(BF16)
    b = _rms(om_ref[...], gm_ref[...]).astype(BF16)
    h_ref[...] = x_ref[...] + (_dot(a, w_ref[0:wl, :]) + _dot(b, w_ref[wl:, :]))


def _out_proj(x2, o_mla, o_moba, g_l, g_m, w_out):
    T, D = x2.shape
    tm = 512
    row = lambda w: pl.BlockSpec((tm, w), lambda i: (i, 0))
    return pl.pallas_call(
        functools.partial(_out_proj_kernel, wl=o_mla.shape[1]),
        grid=(T // tm,),
        in_specs=[row(D), row(o_mla.shape[1]), row(o_moba.shape[1]), _const_spec(g_l.shape),
                  _const_spec(g_m.shape), _const_spec(w_out.shape)],
        out_specs=row(D),
        out_shape=jax.ShapeDtypeStruct((T, D), F32),
        compiler_params=_params("parallel"),
        name="out_proj",
    )(x2, o_mla, o_moba, g_l, g_m, w_out)


def _rms_matmul_kernel(x_ref, g_ref, w_ref, o_ref):
    xn = _rms(x_ref[...], g_ref[...]).astype(BF16)
    o_ref[...] = _dot(xn, w_ref[...]).astype(o_ref.dtype)


def _rms_matmul(x, g, w, out_dtype, *, tm, tn):
    M, K = x.shape
    N = w.shape[1]
    return pl.pallas_call(
        _rms_matmul_kernel,
        grid=(M // tm, N // tn),
        in_specs=[pl.BlockSpec((tm, K), lambda i, j: (i, 0)), pl.BlockSpec((1, K), lambda i, j: (0, 0)),
                  pl.BlockSpec((K, tn), lambda i, j: (0, j))],
        out_specs=pl.BlockSpec((tm, tn), lambda i, j: (i, j)),
        out_shape=jax.ShapeDtypeStruct((M, N), out_dtype),
        compiler_params=_params("parallel", "parallel"),
        name="rms_matmul",
    )(x, g, w)


def _cross_kernel(h_ref, g_ref, wq_ref, kv_ref, wo_ref, o_ref, *, scale):
    h = h_ref[...]
    D = h.shape[1]
    dh = D // CROSS_HEADS
    q = _dot(_rms(h, g_ref[...]).astype(BF16), wq_ref[...]).astype(BF16)
    outs = []
    for hd in range(CROSS_HEADS):
        s = _dot_nt(q[:, hd * dh:(hd + 1) * dh], kv_ref[:, hd * dh:(hd + 1) * dh]) * scale
        outs.append(_softmax_av(s, kv_ref[:, D + hd * dh:D + (hd + 1) * dh]).astype(BF16))
    o_ref[...] = h + _dot(jnp.concatenate(outs, axis=1), wo_ref[...])


def _cross_attention(h1, g_cross, w_cq, kv, w_co, *, seq, mem_len):
    T, D = h1.shape
    tm = 256
    tiles_per_seq = seq // tm
    row = pl.BlockSpec((tm, D), lambda i: (i, 0))
    return pl.pallas_call(
        functools.partial(_cross_kernel, scale=(D // CROSS_HEADS) ** -0.5),
        grid=(T // tm,),
        in_specs=[row, _const_spec(g_cross.shape), _const_spec(w_cq.shape),
                  pl.BlockSpec((mem_len, 2 * D), lambda i: (i // tiles_per_seq, 0)),
                  _const_spec(w_co.shape)],
        out_specs=row,
        out_shape=jax.ShapeDtypeStruct((T, D), F32),
        compiler_params=_params("parallel"),
        name="cross_attention",
    )(h1, g_cross, w_cq, kv, w_co)


def _top16_cols(s, exact):
    n, tm = s.shape
    idx = lax.broadcasted_iota(jnp.int32, (n, tm), 0).astype(F32) if exact else None
    rank = jnp.full((n, tm), float(PEER_TOPK), F32)
    vals = []
    for r in range(PEER_TOPK):
        m = jnp.max(s, axis=0, keepdims=True)
        hit = s == m
        if exact:
            hit = idx == jnp.min(jnp.where(hit, idx, float(n)), axis=0, keepdims=True)
        rank = jnp.where(hit, float(r), rank)
        s = jnp.where(hit, NEG_INF, s)
        vals.append(m)
    kept = jnp.sum(jnp.where(rank < float(PEER_TOPK), 1.0, 0.0), axis=0, keepdims=True)
    return vals, rank, kept == float(PEER_TOPK)


_PAIR_B = [PEER_TOPK // (a + 1) for a in range(8)]


def _pair_counts(v1, v2):
    tm = v1[0].shape[1]
    v1s = jnp.concatenate(v1, axis=0)
    v2s = jnp.concatenate(v2, axis=0)
    pieces, keys = [], []
    for a, nb in enumerate(_PAIR_B):
        rows = -(-nb // 8) * 8
        r = lax.broadcasted_iota(jnp.int32, (rows, tm), 0)
        pieces.append(jnp.where(r < nb, v2s[0:rows, :] + v1[a], NEG_INF))
        keys.append((a * PEER_TOPK + r).astype(F32))
    r = lax.broadcasted_iota(jnp.int32, (8, tm), 0)
    pieces.append(v1s[8:16, :] + v2[0])
    keys.append(((r + 8) * PEER_TOPK).astype(F32))
    cand = jnp.concatenate(pieces, axis=0)
    key = jnp.concatenate(keys, axis=0)
    work = cand
    for _ in range(PEER_TOPK):
        m = jnp.max(work, axis=0, keepdims=True)
        first = jnp.min(jnp.where(work == m, key, float(PEER_TOPK * PEER_TOPK)), axis=0, keepdims=True)
        work = jnp.where(key == first, NEG_INF, work)
    chosen = work != cand
    top = v1[0] + v2[0]
    z = jnp.sum(jnp.where(chosen, jnp.exp(cand - top), 0.0), axis=0, keepdims=True)
    picked = jnp.where(chosen, 1.0, 0.0)
    counts, o = [], 0
    for nb in _PAIR_B:
        rows = -(-nb // 8) * 8
        counts.append(jnp.sum(picked[o:o + rows, :], axis=0, keepdims=True))
        o += rows
    for a in range(8):
        counts.append(picked[o + a:o + a + 1, :])
    return counts, z


Z_E1, Z_CNT, Z_RANK2, Z_E2 = range(4)
Z_GROUPS = PEER_NKEYS // 8


def _peer_retrieve(q_ref, k1_ref, k2_ref, z_ref, exact):
    half = PEER_DKEY // 2
    hi = lax.Precision.HIGHEST
    unresolved = jnp.zeros((1, LANES), F32)
    for c in range(q_ref.shape[0] // LANES):
        ts = slice(c * LANES, (c + 1) * LANES)
        for hd in range(PEER_HEADS):
            o = hd * PEER_DKEY
            s1 = _dot_nt(k1_ref[...], q_ref[ts, o:o + half], precision=hi)
            s2 = _dot_nt(k2_ref[...], q_ref[ts, o + half:o + 2 * half], precision=hi)
            v1, rank1, ok1 = _top16_cols(s1, exact)
            v2, rank2, ok2 = _top16_cols(s2, exact)
            counts, z = _pair_counts(v1, v2)
            unresolved = jnp.where(ok1 & ok2, unresolved, 1.0)
            cnt = jnp.zeros(s1.shape, F32)
            for a in range(PEER_TOPK):
                cnt = jnp.where(rank1 == float(a), counts[a], cnt)
            planes = {Z_E1: jnp.exp(s1 - v1[0]), Z_CNT: cnt, Z_RANK2: rank2, Z_E2: jnp.exp(s2 - v2[0]) / z}
            for p, val in planes.items():
                z_ref[c, hd, :, p] = val.reshape(Z_GROUPS, 8, LANES)
    return unresolved


def _peer_select_kernel(h_ref, g_ref, wq_ref, k1_ref, k2_ref, xn_ref, z_ref, q_ref):
    xn = _rms(h_ref[...], g_ref[...]).astype(BF16)
    xn_ref[...] = xn
    q_ref[...] = _dot(xn, wq_ref[...])
    unresolved = _peer_retrieve(q_ref, k1_ref, k2_ref, z_ref, exact=False)

    @pl.when(jnp.max(unresolved) > 0.0)
    def _():
        _peer_retrieve(q_ref, k1_ref, k2_ref, z_ref, exact=True)


def _peer_select(h2, g_ffn, w_pq, k1, k2):
    T, D = h2.shape
    tm = 256
    row = pl.BlockSpec((tm, D), lambda i: (i, 0))
    z_block = (tm // LANES, PEER_HEADS, Z_GROUPS, 4, 8, LANES)
    return pl.pallas_call(
        _peer_select_kernel,
        grid=(T // tm,),
        in_specs=[row, _const_spec(g_ffn.shape), _const_spec(w_pq.shape), _const_spec(k1.shape),
                  _const_spec(k2.shape)],
        out_specs=[row, pl.BlockSpec(z_block, lambda i: (i, 0, 0, 0, 0, 0))],
        out_shape=[jax.ShapeDtypeStruct((T, D), BF16),
                   jax.ShapeDtypeStruct((T // LANES,) + z_block[1:], F32)],
        scratch_shapes=[pltpu.VMEM((tm, w_pq.shape[1]), F32)],
        compiler_params=_params("parallel"),
        name="peer_select",
    )(h2, g_ffn, w_pq, k1, k2)


GATE_ROWS = 32
MXU_COLS = 256


def _peer_gate_block(at_ref, wt_ref, z_ref, group, sub0, c, sg, *, te):
    n_i1 = te // PEER_NKEYS
    groups_per_block = GATE_ROWS // 8
    gs = slice(sg * groups_per_block, (sg + 1) * groups_per_block)
    gate = [None] * n_i1
    for hd in range(PEER_HEADS):
        rank2 = z_ref[c, hd, gs, Z_RANK2]
        e2 = z_ref[c, hd, gs, Z_E2]
        for r in range(n_i1):
            rs = slice(sub0 + r, sub0 + r + 1)
            e1 = z_ref[c, hd, group, Z_E1, rs, :]
            cnt = z_ref[c, hd, group, Z_CNT, rs, :]
            term = e1 * jnp.where(rank2 < cnt, e2, 0.0)
            gate[r] = term if gate[r] is None else gate[r] + term
    for r in range(n_i1):
        es = slice(r * PEER_NKEYS + sg * GATE_ROWS, r * PEER_NKEYS + (sg + 1) * GATE_ROWS)
        a = at_ref[c, es, :]
        act = 0.5 * a * (1.0 + lax.erf(a * math.sqrt(0.5)))
        wt_ref[c, es, :] = (gate[r].reshape(GATE_ROWS, LANES) * act).astype(BF16)


def _peer_dense_kernel(xn_ref, u_ref, vt_ref, z_ref, h_ref, gf_ref,
                       o_ref, acc_ref, at0_ref, at1_ref, wt0_ref, wt1_ref, *, te, tm, n_tiles):
    j = pl.program_id(1)

    @pl.when(j == 0)
    def _():
        acc_ref[...] = jnp.zeros_like(acc_ref)
        at1_ref[...] = jnp.zeros_like(at1_ref)
        wt0_ref[...] = jnp.zeros_like(wt0_ref)

    rows_per_step = te // PEER_NKEYS
    assert 2 * rows_per_step == 8 and n_tiles % 2 == 0

    n_chunks = tm // LANES
    assert n_chunks == 4 and 2 * LANES == MXU_COLS
    n_sg = PEER_NKEYS // GATE_ROWS
    d_piece = acc_ref.shape[1] // (n_chunks * n_sg // 2)

    def step(at_w, at_r, wt_w, wt_r, tile_b, sub0):
        for c in range(n_chunks):
            for sg in range(n_sg):
                k = c * n_sg + sg
                p, dr = k % 2, slice((k // 2) * d_piece, (k // 2 + 1) * d_piece)
                w = jnp.concatenate([wt_r[2 * p], wt_r[2 * p + 1]], axis=1)
                res = _dot(vt_ref[dr, :], w)
                acc_ref[2 * p, dr, :] += res[:, :LANES]
                acc_ref[2 * p + 1, dr, :] += res[:, LANES:]
                _peer_gate_block(at_r, wt_w, z_ref, tile_b // 2, sub0, c, sg, te=te)
            c0, er = 2 * (c // 2), slice((c % 2) * (te // 2), (c % 2 + 1) * (te // 2))
            res = _dot_nt(u_ref[er, :], xn_ref[c0 * LANES:(c0 + 2) * LANES, :])
            at_w[c0, er, :] = res[:, :LANES]
            at_w[c0 + 1, er, :] = res[:, LANES:]

    @pl.when(j % 2 == 0)
    def _():
        step(at0_ref, at1_ref, wt1_ref, wt0_ref, jnp.maximum(j - 1, 1), rows_per_step)

    @pl.when(j % 2 == 1)
    def _():
        step(at1_ref, at0_ref, wt0_ref, wt1_ref, jnp.minimum(j - 1, n_tiles - 2), 0)

    @pl.when(j == pl.num_programs(1) - 1)
    def _():
        for c in range(n_chunks):
            rows = slice(c * LANES, (c + 1) * LANES)
            o_ref[rows, :] = _rms(h_ref[rows, :] + acc_ref[c].T, gf_ref[...])


PEER_TE = 512


def _peer_dense(xn, u, vt, z, h2, g_final):
    T, D = h2.shape
    tm, te = 512, PEER_TE
    n_tiles = vt.shape[0]
    n_chunks = tm // LANES
    clamp = lambda t: jnp.minimum(jnp.maximum(t, 0), n_tiles - 1)
    row = pl.BlockSpec((tm, D), lambda i, j: (i, 0))
    return pl.pallas_call(
        functools.partial(_peer_dense_kernel, te=te, tm=tm, n_tiles=n_tiles),
        grid=(T // tm, n_tiles + 2),
        in_specs=[row, pl.BlockSpec((te, D), lambda i, j: (clamp(j), 0)),
                  pl.BlockSpec((None, D, te), lambda i, j: (clamp(j - 2), 0, 0)),
                  pl.BlockSpec((n_chunks,) + z.shape[1:], lambda i, j: (i, 0, 0, 0, 0, 0)),
                  row, pl.BlockSpec((1, D), lambda i, j: (0, 0))],
        out_specs=row,
        out_shape=jax.ShapeDtypeStruct((T, D), F32),
        scratch_shapes=[pltpu.VMEM((n_chunks, D, LANES), F32),
                        pltpu.VMEM((n_chunks, te, LANES), F32), pltpu.VMEM((n_chunks, te, LANES), F32),
                        pltpu.VMEM((n_chunks, te, LANES), BF16), pltpu.VMEM((n_chunks, te, LANES), BF16)],
        compiler_params=_params("parallel", "arbitrary"),
        name="peer_dense",
    )(xn, u, vt, z, h2, g_final)


def _layer(h, mem2, tabs, g_attn, w_in, g_q_a, w_uq, g_kv_a, w_ukv, g_mla_o, g_moba_o, w_out, g_cross,
           g_mem, w_cq, w_ck, w_cv, w_co, g_ffn, w_pq, sub_keys1, sub_keys2, expert_u, expert_v,
           g_final, *, batch, seq, mem_len):
    D = h.shape[1]
    row = lambda g: g.reshape(1, -1)
    n_lat = MLA_Q_RANK + MLA_KV_RANK + MLA_ROPE
    w_in_p = jnp.concatenate([w_in[:, :n_lat], jnp.zeros((D, LANES - MLA_ROPE), w_in.dtype),
                              w_in[:, n_lat:]], axis=1).astype(BF16)
    w_uq_p = jnp.pad(w_uq.reshape(MLA_Q_RANK, MLA_HEADS, MLA_NOPE + MLA_ROPE),
                     ((0, 0), (0, 0), (0, MLA_QK_PAD - MLA_NOPE - MLA_ROPE))
                     ).reshape(MLA_Q_RANK, MLA_HEADS * MLA_QK_PAD).astype(BF16)
    w_ukv3 = w_ukv.reshape(MLA_KV_RANK, MLA_HEADS, MLA_NOPE + MLA_V)
    w_uk = w_ukv3[:, :, :MLA_NOPE].reshape(MLA_KV_RANK, MLA_HEADS * MLA_NOPE).astype(BF16)
    w_uv = w_ukv3[:, :, MLA_NOPE:].reshape(MLA_KV_RANK, MLA_HEADS * MLA_V).astype(BF16)

    ql, kl, vl, qm, km, vm, sel = _attn_prep(h, row(g_attn), w_in_p, row(g_q_a), w_uq_p, row(g_kv_a),
                                             w_uk, w_uv, tabs, seq=seq)
    o_mla = _mla_attention(ql, kl, vl, batch=batch, seq=seq)
    o_moba = _moba_attention(qm, km, vm, sel, batch=batch, seq=seq)
    h1 = _out_proj(h, o_mla, o_moba, row(g_mla_o), row(g_moba_o), w_out.astype(BF16))

    w_ckv = jnp.concatenate([w_ck, w_cv], axis=1).astype(BF16)
    kv = _rms_matmul(mem2, row(g_mem), w_ckv, BF16, tm=min(mem2.shape[0], 512), tn=1024)
    h2 = _cross_attention(h1, row(g_cross), w_cq.astype(BF16), kv, w_co.astype(BF16), seq=seq, mem_len=mem_len)

    xn, z = _peer_select(h2, row(g_ffn), w_pq.astype(BF16), sub_keys1, sub_keys2)
    vt = expert_v.reshape(-1, PEER_TE, D).transpose(0, 2, 1).astype(BF16)
    return _peer_dense(xn, expert_u.astype(BF16), vt, z, h2, row(g_final))


def kernel(x, mem, positions, g_attn, w_in, g_q_a, w_uq, g_kv_a, w_ukv, g_mla_o, g_moba_o, w_out, g_cross,
           g_mem, w_cq, w_ck, w_cv, w_co, g_ffn, w_pq, sub_keys1, sub_keys2, expert_u, expert_v, g_final):
    B, S, D = x.shape
    M = mem.shape[1]
    depth = w_in.shape[0]
    assert depth == 1, "the final norm is fused into the layer's last kernel"
    tabs = _rope_tables(positions.reshape(B * S, 1))
    out = _layer(x.reshape(B * S, D), mem.reshape(B * M, D), tabs, g_attn[0], w_in[0], g_q_a[0], w_uq[0],
                 g_kv_a[0], w_ukv[0], g_mla_o[0], g_moba_o[0], w_out[0], g_cross[0], g_mem[0], w_cq[0],
                 w_ck[0], w_cv[0], w_co[0], g_ffn[0], w_pq[0], sub_keys1[0], sub_keys2[0], expert_u[0],
                 expert_v[0], g_final, batch=B, seq=S, mem_len=M)
    return out.reshape(B, S, D)
```

```python
import functools
import math

import jax
import jax.numpy as jnp
from jax import lax
from jax.experimental import pallas as pl
from jax.experimental.pallas import tpu as pltpu

F32 = jnp.float32
BF16 = jnp.bfloat16

EPS = 1e-6
ROPE_THETA = 10000.0
MLA_HEADS = 8
MLA_Q_RANK = 512
MLA_KV_RANK = 256
MLA_NOPE = 128
MLA_ROPE = 64
MLA_V = 128
MLA_QK_PAD = 256
MOBA_HEADS = 8
MOBA_HEAD_DIM = 128
MOBA_BLOCK = 256
MOBA_TOPK = 3
CROSS_HEADS = 4
PEER_HEADS = 8
PEER_NKEYS = 128
PEER_DKEY = 256
PEER_TOPK = 16

LANES = 128
VMEM_LIMIT = 56 * 1024 * 1024

NEG_INF = float("-inf")
NT_DIMS = (((1,), (1,)), ((), ()))


def _rms(x, g):
    ms = jnp.mean(x * x, axis=-1, keepdims=True)
    return x * lax.rsqrt(ms + EPS) * g


def _dot(a, b):
    return jnp.dot(a, b, preferred_element_type=F32)


def _dot_nt(a, b, precision=None):
    return lax.dot_general(a, b, NT_DIMS, precision=precision, preferred_element_type=F32)


def _const_spec(shape):
    nd = len(shape)
    return pl.BlockSpec(shape, lambda *_: (0,) * nd, pipeline_mode=pl.Buffered(1))


def _params(*sem):
    return pltpu.CompilerParams(dimension_semantics=sem, vmem_limit_bytes=VMEM_LIMIT)


def _rope_table_kernel(pos_ref, cm_ref, sm_ref, cl_ref, sl_ref):
    pos = pos_ref[...].astype(F32)
    lane = lax.broadcasted_iota(jnp.int32, (1, LANES), 1)
    half = MOBA_HEAD_DIM // 2
    inv = jnp.exp((lane % half).astype(F32) * (-2.0 * math.log(ROPE_THETA) / MOBA_HEAD_DIM))
    ang = pos * inv
    cm_ref[...] = jnp.cos(ang)
    sn = jnp.sin(ang)
    sm_ref[...] = jnp.where(lane < half, -sn, sn)
    half = MLA_ROPE // 2
    inv = jnp.exp((lane % half).astype(F32) * (-2.0 * math.log(ROPE_THETA) / MLA_ROPE))
    ang = pos * inv
    valid = lane < MLA_ROPE
    cl_ref[...] = jnp.where(valid, jnp.cos(ang), 0.0)
    sn = jnp.sin(ang)
    sl_ref[...] = jnp.where(lane < half, -sn, jnp.where(valid, sn, 0.0))


def _rope_tables(pos_col):
    T = pos_col.shape[0]
    tm = 512
    out = jax.ShapeDtypeStruct((T, LANES), F32)
    spec = pl.BlockSpec((tm, LANES), lambda i: (i, 0))
    return pl.pallas_call(
        _rope_table_kernel,
        grid=(T // tm,),
        in_specs=[pl.BlockSpec((tm, 1), lambda i: (i, 0))],
        out_specs=[spec] * 4,
        out_shape=[out] * 4,
        compiler_params=_params("parallel"),
        name="rope_tables",
    )(pos_col)


def _rope_moba(x, cos, sin):
    return x * cos + pltpu.roll(x, MOBA_HEAD_DIM // 2, 1) * sin


def _rope_mla(x, cos, sin, lane):
    half = MLA_ROPE // 2
    partner = jnp.where(lane < half, pltpu.roll(x, LANES - half, 1), pltpu.roll(x, half, 1))
    return x * cos + partner * sin


C_COLS = MLA_Q_RANK + MLA_KV_RANK + LANES
MOBA_W = MOBA_HEADS * MOBA_HEAD_DIM


def _attn_prep_kernel(x_ref, g_ref, w_in_ref, gq_ref, wuq_ref, gkv_ref, wuk_ref, wuv_ref,
                      cm_ref, sm_ref, cl_ref, sl_ref,
                      ql_ref, kl_ref, vl_ref, qm_ref, km_ref, vm_ref, sel_ref,
                      kbar_ref, qmf_ref, kmf_ref, *, tm, tiles_per_seq, n_blocks):
    i = pl.program_id(0)

    @pl.when(i == 0)
    def _():
        kbar_ref[...] = jnp.zeros_like(kbar_ref)

    xn = _rms(x_ref[...], g_ref[...]).astype(BF16)
    lane = lax.broadcasted_iota(jnp.int32, (1, LANES), 1)
    cl, sl = cl_ref[...], sl_ref[...]
    cm, sm = cm_ref[...], sm_ref[...]

    c = _dot(xn, w_in_ref[:, 0:C_COLS])
    cq = _rms(c[:, 0:MLA_Q_RANK], gq_ref[...]).astype(BF16)
    q = _dot(cq, wuq_ref[...])
    ckv = _rms(c[:, MLA_Q_RANK:MLA_Q_RANK + MLA_KV_RANK], gkv_ref[...]).astype(BF16)
    kn = _dot(ckv, wuk_ref[...])
    vl_ref[...] = _dot(ckv, wuv_ref[...]).astype(BF16)
    kr = _rope_mla(c[:, MLA_Q_RANK + MLA_KV_RANK:C_COLS], cl, sl, lane).astype(BF16)
    for h in range(MLA_HEADS):
        o = h * MLA_QK_PAD
        ql_ref[:, o:o + LANES] = q[:, o:o + LANES].astype(BF16)
        ql_ref[:, o + LANES:o + 2 * LANES] = _rope_mla(q[:, o + LANES:o + 2 * LANES], cl, sl, lane).astype(BF16)
        kl_ref[:, o:o + LANES] = kn[:, h * LANES:(h + 1) * LANES].astype(BF16)
        kl_ref[:, o + LANES:o + 2 * LANES] = kr

    o0 = C_COLS
    qm = _dot(xn, w_in_ref[:, o0:o0 + MOBA_W])
    km = _dot(xn, w_in_ref[:, o0 + MOBA_W:o0 + 2 * MOBA_W])
    vm_ref[...] = _dot(xn, w_in_ref[:, o0 + 2 * MOBA_W:o0 + 3 * MOBA_W]).astype(BF16)
    for h in range(MOBA_HEADS):
        sl_h = slice(h * LANES, (h + 1) * LANES)
        qr = _rope_moba(qm[:, sl_h], cm, sm)
        kr_h = _rope_moba(km[:, sl_h], cm, sm)
        qmf_ref[:, sl_h] = qr
        kmf_ref[:, sl_h] = kr_h
        qm_ref[:, sl_h] = qr.astype(BF16)
        km_ref[:, sl_h] = kr_h.astype(BF16)

    blk0 = (i % tiles_per_seq) * (tm // MOBA_BLOCK)
    for s in range(tm // MOBA_BLOCK):
        kbar_ref[pl.ds(blk0 + s, 1), :] = jnp.mean(
            kmf_ref[s * MOBA_BLOCK:(s + 1) * MOBA_BLOCK, :], axis=0, keepdims=True)
    tok = lax.broadcasted_iota(jnp.int32, (n_blocks, tm), 1)
    own = blk0 + tok // MOBA_BLOCK
    blk = lax.broadcasted_iota(jnp.int32, (n_blocks, tm), 0)
    past = blk < own
    rows = []
    for h in range(MOBA_HEADS):
        sl_h = slice(h * LANES, (h + 1) * LANES)
        gate = _dot_nt(kbar_ref[:, sl_h], qmf_ref[:, sl_h], precision=lax.Precision.HIGHEST)
        gate = jnp.where(past, gate, NEG_INF)
        ahead = jnp.zeros((n_blocks, tm), jnp.int32)
        for m in range(n_blocks):
            gm = gate[m:m + 1, :]
            beats = (gm > gate) | ((gm == gate) & (m < blk))
            ahead = ahead + beats.astype(jnp.int32)
        rows.append(jnp.where(past & (ahead < MOBA_TOPK), 1.0, 0.0).astype(F32))
    pad = jnp.zeros((LANES - MOBA_HEADS * n_blocks, tm), F32)
    sel_ref[...] = jnp.concatenate(rows + [pad], axis=0).T


def _attn_prep(x2, g_attn, w_in_p, g_q, w_uq_p, g_kv, w_uk, w_uv, tabs, *, seq):
    T, D = x2.shape
    tm = 512
    n_blocks = seq // MOBA_BLOCK
    assert seq % tm == 0 and tm % MOBA_BLOCK == 0 and MOBA_HEADS * n_blocks <= LANES
    cm, sm, cl, sl = tabs
    row = lambda w: pl.BlockSpec((tm, w), lambda i: (i, 0))
    kern = functools.partial(_attn_prep_kernel, tm=tm, tiles_per_seq=seq // tm, n_blocks=n_blocks)
    outs = [(MLA_HEADS * MLA_QK_PAD, BF16), (MLA_HEADS * MLA_QK_PAD, BF16), (MLA_HEADS * MLA_V, BF16),
            (MOBA_W, BF16), (MOBA_W, BF16), (MOBA_W, BF16), (LANES, F32)]
    return pl.pallas_call(
        kern,
        grid=(T // tm,),
        in_specs=[row(D), _const_spec(g_attn.shape), _const_spec(w_in_p.shape), _const_spec(g_q.shape),
                  _const_spec(w_uq_p.shape), _const_spec(g_kv.shape), _const_spec(w_uk.shape),
                  _const_spec(w_uv.shape), row(LANES), row(LANES), row(LANES), row(LANES)],
        out_specs=[row(w) for w, _ in outs],
        out_shape=[jax.ShapeDtypeStruct((T, w), dt) for w, dt in outs],
        scratch_shapes=[pltpu.VMEM((n_blocks, MOBA_W), F32), pltpu.VMEM((tm, MOBA_W), F32),
                        pltpu.VMEM((tm, MOBA_W), F32)],
        compiler_params=_params("arbitrary"),
        name="attn_prep",
    )(x2, g_attn, w_in_p, g_q, w_uq_p, g_kv, w_uk, w_uv, cm, sm, cl, sl)


ATTN_TQ = 256


def _softmax_av(s, v):
    m = jnp.max(s, axis=-1, keepdims=True)
    e = jnp.exp(s - m)
    l = jnp.sum(e, axis=-1, keepdims=True)
    return _dot(e.astype(BF16), v) / l


def _causal_bias(tq):
    r = lax.broadcasted_iota(jnp.int32, (tq, tq), 0)
    c = lax.broadcasted_iota(jnp.int32, (tq, tq), 1)
    return jnp.where(c <= r, 0.0, NEG_INF).astype(F32)


def _pipelined_tiles(scores, attend, o_ref, n_tiles, tq):
    s_next = scores(0)
    for i in range(n_tiles):
        s = s_next
        if i + 1 < n_tiles:
            s_next = scores(i + 1)
        o_ref[i * tq:(i + 1) * tq, :] = attend(s, (i + 1) * tq)


def _mla_attn_kernel(q_ref, k_ref, v_ref, o_ref, *, seq, scale):
    tq = ATTN_TQ
    diag = _causal_bias(tq)

    def scores(i):
        n = (i + 1) * tq
        s = _dot_nt(q_ref[i * tq:n, :], k_ref[0:n, :]) * scale
        return s + jnp.concatenate([jnp.zeros((tq, i * tq), F32), diag], axis=1) if i else s + diag

    _pipelined_tiles(scores, lambda s, n: _softmax_av(s, v_ref[0:n, :]), o_ref, seq // tq, tq)


def _mla_attention(ql, kl, vl, *, batch, seq):
    T = ql.shape[0]
    scale = (MLA_NOPE + MLA_ROPE) ** -0.5
    return pl.pallas_call(
        functools.partial(_mla_attn_kernel, seq=seq, scale=scale),
        grid=(batch, MLA_HEADS),
        in_specs=[pl.BlockSpec((seq, MLA_QK_PAD), lambda b, h: (b, h)),
                  pl.BlockSpec((seq, MLA_QK_PAD), lambda b, h: (b, h)),
                  pl.BlockSpec((seq, MLA_V), lambda b, h: (b, h))],
        out_specs=pl.BlockSpec((seq, MLA_V), lambda b, h: (b, h)),
        out_shape=jax.ShapeDtypeStruct((T, MLA_HEADS * MLA_V), F32),
        compiler_params=_params("parallel", "parallel"),
        name="mla_attention",
    )(ql, kl, vl)


def _moba_attn_kernel(q_ref, k_ref, v_ref, sel_ref, o_ref, *, seq, scale, n_blocks):
    tq = ATTN_TQ
    h = pl.program_id(1)
    lane = lax.broadcasted_iota(jnp.int32, (1, LANES), 1)
    diag = _causal_bias(tq)

    def scores(i):
        n = (i + 1) * tq
        s = _dot_nt(q_ref[i * tq:n, :], k_ref[0:n, :]) * scale
        sel = sel_ref[i * tq:n, :]
        pieces = []
        for b in range(i):
            chosen = jnp.sum(jnp.where(lane == h * n_blocks + b, sel, 0.0), axis=-1, keepdims=True)
            pieces.append(jnp.broadcast_to(jnp.where(chosen > 0.0, 0.0, NEG_INF), (tq, tq)))
        return s + (jnp.concatenate(pieces + [diag], axis=1) if pieces else diag)

    _pipelined_tiles(scores, lambda s, n: _softmax_av(s, v_ref[0:n, :]), o_ref, seq // tq, tq)


def _moba_attention(qm, km, vm, sel, *, batch, seq):
    T = qm.shape[0]
    assert ATTN_TQ == MOBA_BLOCK
    hd = pl.BlockSpec((seq, MOBA_HEAD_DIM), lambda b, h: (b, h))
    return pl.pallas_call(
        functools.partial(_moba_attn_kernel, seq=seq, scale=MOBA_HEAD_DIM ** -0.5,
                          n_blocks=seq // MOBA_BLOCK),
        grid=(batch, MOBA_HEADS),
        in_specs=[hd, hd, hd, pl.BlockSpec((seq, LANES), lambda b, h: (b, 0))],
        out_specs=hd,
        out_shape=jax.ShapeDtypeStruct((T, MOBA_W), F32),
        compiler_params=_params("parallel", "parallel"),
        name="moba_attention",
    )(qm, km, vm, sel)


def _out_proj_kernel(x_ref, ol_ref, om_ref, gl_ref, gm_ref, w_ref, h_ref, *, wl):
    a = _rms(ol_ref[...], gl_ref[...]).astype(BF16)
    b = _rms(om_ref[...], gm_ref[...]).astype(BF16)
    h_ref[...] = x_ref[...] + (_dot(a, w_ref[0:wl, :]) + _dot(b, w_ref[wl:, :]))


def _out_proj(x2, o_mla, o_moba, g_l, g_m, w_out):
    T, D = x2.shape
    tm = 512
    row = lambda w: pl.BlockSpec((tm, w), lambda i: (i, 0))
    return pl.pallas_call(
        functools.partial(_out_proj_kernel, wl=o_mla.shape[1]),
        grid=(T // tm,),
        in_specs=[row(D), row(o_mla.shape[1]), row(o_moba.shape[1]), _const_spec(g_l.shape),
                  _const_spec(g_m.shape), _const_spec(w_out.shape)],
        out_specs=row(D),
        out_shape=jax.ShapeDtypeStruct((T, D), F32),
        compiler_params=_params("parallel"),
        name="out_proj",
    )(x2, o_mla, o_moba, g_l, g_m, w_out)


def _rms_matmul_kernel(x_ref, g_ref, w_ref, o_ref):
    xn = _rms(x_ref[...], g_ref[...]).astype(BF16)
    o_ref[...] = _dot(xn, w_ref[...]).astype(o_ref.dtype)


def _rms_matmul(x, g, w, out_dtype, *, tm, tn):
    M, K = x.shape
    N = w.shape[1]
    return pl.pallas_call(
        _rms_matmul_kernel,
        grid=(M // tm, N // tn),
        in_specs=[pl.BlockSpec((tm, K), lambda i, j: (i, 0)), pl.BlockSpec((1, K), lambda i, j: (0, 0)),
                  pl.BlockSpec((K, tn), lambda i, j: (0, j))],
        out_specs=pl.BlockSpec((tm, tn), lambda i, j: (i, j)),
        out_shape=jax.ShapeDtypeStruct((M, N), out_dtype),
        compiler_params=_params("parallel", "parallel"),
        name="rms_matmul",
    )(x, g, w)


def _cross_kernel(h_ref, g_ref, wq_ref, kv_ref, wo_ref, o_ref, *, scale):
    h = h_ref[...]
    D = h.shape[1]
    dh = D // CROSS_HEADS
    q = _dot(_rms(h, g_ref[...]).astype(BF16), wq_ref[...]).astype(BF16)
    outs = []
    for hd in range(CROSS_HEADS):
        s = _dot_nt(q[:, hd * dh:(hd + 1) * dh], kv_ref[:, hd * dh:(hd + 1) * dh]) * scale
        outs.append(_softmax_av(s, kv_ref[:, D + hd * dh:D + (hd + 1) * dh]).astype(BF16))
    o_ref[...] = h + _dot(jnp.concatenate(outs, axis=1), wo_ref[...])


def _cross_attention(h1, g_cross, w_cq, kv, w_co, *, seq, mem_len):
    T, D = h1.shape
    tm = 256
    tiles_per_seq = seq // tm
    row = pl.BlockSpec((tm, D), lambda i: (i, 0))
    return pl.pallas_call(
        functools.partial(_cross_kernel, scale=(D // CROSS_HEADS) ** -0.5),
        grid=(T // tm,),
        in_specs=[row, _const_spec(g_cross.shape), _const_spec(w_cq.shape),
                  pl.BlockSpec((mem_len, 2 * D), lambda i: (i // tiles_per_seq, 0)),
                  _const_spec(w_co.shape)],
        out_specs=row,
        out_shape=jax.ShapeDtypeStruct((T, D), F32),
        compiler_params=_params("parallel"),
        name="cross_attention",
    )(h1, g_cross, w_cq, kv, w_co)


def _top16_cols(s, exact):
    n, tm = s.shape
    idx = lax.broadcasted_iota(jnp.int32, (n, tm), 0).astype(F32) if exact else None
    rank = jnp.full((n, tm), float(PEER_TOPK), F32)
    vals = []
    for r in range(PEER_TOPK):
        m = jnp.max(s, axis=0, keepdims=True)
        hit = s == m
        if exact:
            hit = idx == jnp.min(jnp.where(hit, idx, float(n)), axis=0, keepdims=True)
        rank = jnp.where(hit, float(r), rank)
        s = jnp.where(hit, NEG_INF, s)
        vals.append(m)
    kept = jnp.sum(jnp.where(rank < float(PEER_TOPK), 1.0, 0.0), axis=0, keepdims=True)
    return vals, rank, kept == float(PEER_TOPK)


_PAIR_B = [PEER_TOPK // (a + 1) for a in range(8)]


def _pair_counts(v1, v2):
    tm = v1[0].shape[1]
    v1s = jnp.concatenate(v1, axis=0)
    v2s = jnp.concatenate(v2, axis=0)
    pieces, keys = [], []
    for a, nb in enumerate(_PAIR_B):
        rows = -(-nb // 8) * 8
        r = lax.broadcasted_iota(jnp.int32, (rows, tm), 0)
        pieces.append(jnp.where(r < nb, v2s[0:rows, :] + v1[a], NEG_INF))
        keys.append((a * PEER_TOPK + r).astype(F32))
    r = lax.broadcasted_iota(jnp.int32, (8, tm), 0)
    pieces.append(v1s[8:16, :] + v2[0])
    keys.append(((r + 8) * PEER_TOPK).astype(F32))
    cand = jnp.concatenate(pieces, axis=0)
    key = jnp.concatenate(keys, axis=0)
    work = cand
    for _ in range(PEER_TOPK):
        m = jnp.max(work, axis=0, keepdims=True)
        first = jnp.min(jnp.where(work == m, key, float(PEER_TOPK * PEER_TOPK)), axis=0, keepdims=True)
        work = jnp.where(key == first, NEG_INF, work)
    chosen = work != cand
    top = v1[0] + v2[0]
    z = jnp.sum(jnp.where(chosen, jnp.exp(cand - top), 0.0), axis=0, keepdims=True)
    picked = jnp.where(chosen, 1.0, 0.0)
    counts, o = [], 0
    for nb in _PAIR_B:
        rows = -(-nb // 8) * 8
        counts.append(jnp.sum(picked[o:o + rows, :], axis=0, keepdims=True))
        o += rows
    for a in range(8):
        counts.append(picked[o + a:o + a + 1, :])
    return counts, z


Z_E1, Z_CNT, Z_RANK2, Z_E2 = range(4)
Z_GROUPS = PEER_NKEYS // 8


def _peer_retrieve_unit(q_ref, k1_ref, k2_ref, z_ref, c, hd, exact):
    half = PEER_DKEY // 2
    hi = lax.Precision.HIGHEST
    ts = slice(c * LANES, (c + 1) * LANES)
    o = hd * PEER_DKEY
    s1 = _dot_nt(k1_ref[...], q_ref[ts, o:o + half], precision=hi)
    s2 = _dot_nt(k2_ref[...], q_ref[ts, o + half:o + 2 * half], precision=hi)
    v1, rank1, ok1 = _top16_cols(s1, exact)
    v2, rank2, ok2 = _top16_cols(s2, exact)
    counts, z = _pair_counts(v1, v2)
    cnt = jnp.zeros(s1.shape, F32)
    for a in range(PEER_TOPK):
        cnt = jnp.where(rank1 == float(a), counts[a], cnt)
    planes = {Z_E1: jnp.exp(s1 - v1[0]), Z_CNT: cnt, Z_RANK2: rank2, Z_E2: jnp.exp(s2 - v2[0]) / z}
    for p, val in planes.items():
        z_ref[c, hd, :, p] = val.reshape(Z_GROUPS, 8, LANES)
    return jnp.where(ok1 & ok2, 0.0, 1.0)


def _peer_select_kernel(h_ref, g_ref, wq_ref, k1_ref, k2_ref, xn_ref, z_ref, q_ref):
    xn = _rms(h_ref[...], g_ref[...]).astype(BF16)
    xn_ref[...] = xn
    q_ref[...] = _dot(xn, wq_ref[...])
    units = [(c, hd) for c in range(q_ref.shape[0] // LANES) for hd in range(PEER_HEADS)]
    redo = [jnp.max(_peer_retrieve_unit(q_ref, k1_ref, k2_ref, z_ref, c, hd, exact=False)) > 0.0
            for c, hd in units]
    for (c, hd), flag in zip(units, redo):
        @pl.when(flag)
        def _(c=c, hd=hd):
            _peer_retrieve_unit(q_ref, k1_ref, k2_ref, z_ref, c, hd, exact=True)


def _peer_select(h2, g_ffn, w_pq, k1, k2):
    T, D = h2.shape
    tm = 256
    row = pl.BlockSpec((tm, D), lambda i: (i, 0))
    z_block = (tm // LANES, PEER_HEADS, Z_GROUPS, 4, 8, LANES)
    return pl.pallas_call(
        _peer_select_kernel,
        grid=(T // tm,),
        in_specs=[row, _const_spec(g_ffn.shape), _const_spec(w_pq.shape), _const_spec(k1.shape),
                  _const_spec(k2.shape)],
        out_specs=[row, pl.BlockSpec(z_block, lambda i: (i, 0, 0, 0, 0, 0))],
        out_shape=[jax.ShapeDtypeStruct((T, D), BF16),
                   jax.ShapeDtypeStruct((T // LANES,) + z_block[1:], F32)],
        scratch_shapes=[pltpu.VMEM((tm, w_pq.shape[1]), F32)],
        compiler_params=_params("parallel"),
        name="peer_select",
    )(h2, g_ffn, w_pq, k1, k2)


GATE_ROWS = 32
MXU_COLS = 256


def _peer_gate_block(at_ref, wt_ref, z_ref, group, sub0, c, sg, *, te):
    n_i1 = te // PEER_NKEYS
    groups_per_block = GATE_ROWS // 8
    gs = slice(sg * groups_per_block, (sg + 1) * groups_per_block)
    gate = [None] * n_i1
    for hd in range(PEER_HEADS):
        rank2 = z_ref[c, hd, gs, Z_RANK2]
        e2 = z_ref[c, hd, gs, Z_E2]
        for r in range(n_i1):
            rs = slice(sub0 + r, sub0 + r + 1)
            e1 = z_ref[c, hd, group, Z_E1, rs, :]
            cnt = z_ref[c, hd, group, Z_CNT, rs, :]
            term = e1 * jnp.where(rank2 < cnt, e2, 0.0)
            gate[r] = term if gate[r] is None else gate[r] + term
    for r in range(n_i1):
        es = slice(r * PEER_NKEYS + sg * GATE_ROWS, r * PEER_NKEYS + (sg + 1) * GATE_ROWS)
        a = at_ref[c, es, :]
        act = 0.5 * a * (1.0 + lax.erf(a * math.sqrt(0.5)))
        wt_ref[c, es, :] = (gate[r].reshape(GATE_ROWS, LANES) * act).astype(BF16)


def _peer_dense_kernel(xn_ref, u_ref, vt_ref, z_ref, h_ref, gf_ref,
                       o_ref, acc_ref, at0_ref, at1_ref, wt0_ref, wt1_ref, *, te, tm, n_tiles):
    j = pl.program_id(1)

    @pl.when(j == 0)
    def _():
        acc_ref[...] = jnp.zeros_like(acc_ref)
        at1_ref[...] = jnp.zeros_like(at1_ref)
        wt0_ref[...] = jnp.zeros_like(wt0_ref)

    rows_per_step = te // PEER_NKEYS
    assert 2 * rows_per_step == 8 and n_tiles % 2 == 0

    n_chunks = tm // LANES
    assert n_chunks == 4 and 2 * LANES == MXU_COLS
    n_sg = PEER_NKEYS // GATE_ROWS
    d_piece = acc_ref.shape[1] // (n_chunks * n_sg // 2)

    def step(at_w, at_r, wt_w, wt_r, tile_b, sub0):
        for c in range(n_chunks):
            for sg in range(n_sg):
                k = c * n_sg + sg
                p, dr = k % 2, slice((k // 2) * d_piece, (k // 2 + 1) * d_piece)
                w = jnp.concatenate([wt_r[2 * p], wt_r[2 * p + 1]], axis=1)
                res = _dot(vt_ref[dr, :], w)
                acc_ref[2 * p, dr, :] += res[:, :LANES]
                acc_ref[2 * p + 1, dr, :] += res[:, LANES:]
                _peer_gate_block(at_r, wt_w, z_ref, tile_b // 2, sub0, c, sg, te=te)
            c0, er = 2 * (c // 2), slice((c % 2) * (te // 2), (c % 2 + 1) * (te // 2))
            res = _dot_nt(u_ref[er, :], xn_ref[c0 * LANES:(c0 + 2) * LANES, :])
            at_w[c0, er, :] = res[:, :LANES]
            at_w[c0 + 1, er, :] = res[:, LANES:]

    @pl.when(j % 2 == 0)
    def _():
        step(at0_ref, at1_ref, wt1_ref, wt0_ref, jnp.maximum(j - 1, 1), rows_per_step)

    @pl.when(j % 2 == 1)
    def _():
        step(at1_ref, at0_ref, wt0_ref, wt1_ref, jnp.minimum(j - 1, n_tiles - 2), 0)

    @pl.when(j == pl.num_programs(1) - 1)
    def _():
        for c in range(n_chunks):
            rows = slice(c * LANES, (c + 1) * LANES)
            o_ref[rows, :] = _rms(h_ref[rows, :] + acc_ref[c].T, gf_ref[...])


PEER_TE = 512


def _peer_dense(xn, u, vt, z, h2, g_final):
    T, D = h2.shape
    tm, te = 512, PEER_TE
    n_tiles = vt.shape[0]
    n_chunks = tm // LANES
    clamp = lambda t: jnp.minimum(jnp.maximum(t, 0), n_tiles - 1)
    row = pl.BlockSpec((tm, D), lambda i, j: (i, 0))
    return pl.pallas_call(
        functools.partial(_peer_dense_kernel, te=te, tm=tm, n_tiles=n_tiles),
        grid=(T // tm, n_tiles + 2),
        in_specs=[row, pl.BlockSpec((te, D), lambda i, j: (clamp(j), 0)),
                  pl.BlockSpec((None, D, te), lambda i, j: (clamp(j - 2), 0, 0)),
                  pl.BlockSpec((n_chunks,) + z.shape[1:], lambda i, j: (i, 0, 0, 0, 0, 0)),
                  row, pl.BlockSpec((1, D), lambda i, j: (0, 0))],
        out_specs=row,
        out_shape=jax.ShapeDtypeStruct((T, D), F32),
        scratch_shapes=[pltpu.VMEM((n_chunks, D, LANES), F32),
                        pltpu.VMEM((n_chunks, te, LANES), F32), pltpu.VMEM((n_chunks, te, LANES), F32),
                        pltpu.VMEM((n_chunks, te, LANES), BF16), pltpu.VMEM((n_chunks, te, LANES), BF16)],
        compiler_params=_params("parallel", "arbitrary"),
        name="peer_dense",
    )(xn, u, vt, z, h2, g_final)


def _layer(h, mem2, tabs, g_attn, w_in, g_q_a, w_uq, g_kv_a, w_ukv, g_mla_o, g_moba_o, w_out, g_cross,
           g_mem, w_cq, w_ck, w_cv, w_co, g_ffn, w_pq, sub_keys1, sub_keys2, expert_u, expert_v,
           g_final, *, batch, seq, mem_len):
    D = h.shape[1]
    row = lambda g: g.reshape(1, -1)
    n_lat = MLA_Q_RANK + MLA_KV_RANK + MLA_ROPE
    w_in_p = jnp.concatenate([w_in[:, :n_lat], jnp.zeros((D, LANES - MLA_ROPE), w_in.dtype),
                              w_in[:, n_lat:]], axis=1).astype(BF16)
    w_uq_p = jnp.pad(w_uq.reshape(MLA_Q_RANK, MLA_HEADS, MLA_NOPE + MLA_ROPE),
                     ((0, 0), (0, 0), (0, MLA_QK_PAD - MLA_NOPE - MLA_ROPE))
                     ).reshape(MLA_Q_RANK, MLA_HEADS * MLA_QK_PAD).astype(BF16)
    w_ukv3 = w_ukv.reshape(MLA_KV_RANK, MLA_HEADS, MLA_NOPE + MLA_V)
    w_uk = w_ukv3[:, :, :MLA_NOPE].reshape(MLA_KV_RANK, MLA_HEADS * MLA_NOPE).astype(BF16)
    w_uv = w_ukv3[:, :, MLA_NOPE:].reshape(MLA_KV_RANK, MLA_HEADS * MLA_V).astype(BF16)

    ql, kl, vl, qm, km, vm, sel = _attn_prep(h, row(g_attn), w_in_p, row(g_q_a), w_uq_p, row(g_kv_a),
                                             w_uk, w_uv, tabs, seq=seq)
    o_mla = _mla_attention(ql, kl, vl, batch=batch, seq=seq)
    o_moba = _moba_attention(qm, km, vm, sel, batch=batch, seq=seq)
    h1 = _out_proj(h, o_mla, o_moba, row(g_mla_o), row(g_moba_o), w_out.astype(BF16))

    w_ckv = jnp.concatenate([w_ck, w_cv], axis=1).astype(BF16)
    kv = _rms_matmul(mem2, row(g_mem), w_ckv, BF16, tm=min(mem2.shape[0], 512), tn=1024)
    h2 = _cross_attention(h1, row(g_cross), w_cq.astype(BF16), kv, w_co.astype(BF16), seq=seq, mem_len=mem_len)

    xn, z = _peer_select(h2, row(g_ffn), w_pq.astype(BF16), sub_keys1, sub_keys2)
    vt = expert_v.reshape(-1, PEER_TE, D).transpose(0, 2, 1).astype(BF16)
    return _peer_dense(xn, expert_u.astype(BF16), vt, z, h2, row(g_final))


def kernel(x, mem, positions, g_attn, w_in, g_q_a, w_uq, g_kv_a, w_ukv, g_mla_o, g_moba_o, w_out, g_cross,
           g_mem, w_cq, w_ck, w_cv, w_co, g_ffn, w_pq, sub_keys1, sub_keys2, expert_u, expert_v, g_final):
    B, S, D = x.shape
    M = mem.shape[1]
    depth = w_in.shape[0]
    assert depth == 1, "the final norm is fused into the layer's last kernel"
    tabs = _rope_tables(positions.reshape(B * S, 1))
    out = _layer(x.reshape(B * S, D), mem.reshape(B * M, D), tabs, g_attn[0], w_in[0], g_q_a[0], w_uq[0],
                 g_kv_a[0], w_ukv[0], g_mla_o[0], g_moba_o[0], w_out[0], g_cross[0], g_mem[0], w_cq[0],
                 w_ck[0], w_cv[0], w_co[0], g_ffn[0], w_pq[0], sub_keys1[0], sub_keys2[0], expert_u[0],
                 expert_v[0], g_final, batch=B, seq=S, mem_len=M)
    return out.reshape(B, S, D)
```

```python
import functools
import math

import jax
import jax.numpy as jnp
from jax import lax
from jax.experimental import pallas as pl
from jax.experimental.pallas import tpu as pltpu

F32 = jnp.float32
BF16 = jnp.bfloat16

EPS = 1e-6
ROPE_THETA = 10000.0
MLA_HEADS = 8
MLA_Q_RANK = 512
MLA_KV_RANK = 256
MLA_NOPE = 128
MLA_ROPE = 64
MLA_V = 128
MLA_QK_PAD = 256
MOBA_HEADS = 8
MOBA_HEAD_DIM = 128
MOBA_BLOCK = 256
MOBA_TOPK = 3
CROSS_HEADS = 4
PEER_HEADS = 8
PEER_NKEYS = 128
PEER_DKEY = 256
PEER_TOPK = 16

LANES = 128
VMEM_LIMIT = 56 * 1024 * 1024

NEG_INF = float("-inf")
NT_DIMS = (((1,), (1,)), ((), ()))


def _rms(x, g):
    ms = jnp.mean(x * x, axis=-1, keepdims=True)
    return x * lax.rsqrt(ms + EPS) * g


def _dot(a, b):
    return jnp.dot(a, b, preferred_element_type=F32)


def _dot_nt(a, b, precision=None):
    return lax.dot_general(a, b, NT_DIMS, precision=precision, preferred_element_type=F32)


def _const_spec(shape):
    nd = len(shape)
    return pl.BlockSpec(shape, lambda *_: (0,) * nd, pipeline_mode=pl.Buffered(1))


def _params(*sem):
    return pltpu.CompilerParams(dimension_semantics=sem, vmem_limit_bytes=VMEM_LIMIT)


def _rope_table_kernel(pos_ref, cm_ref, sm_ref, cl_ref, sl_ref):
    pos = pos_ref[...].astype(F32)
    lane = lax.broadcasted_iota(jnp.int32, (1, LANES), 1)
    half = MOBA_HEAD_DIM // 2
    inv = jnp.exp((lane % half).astype(F32) * (-2.0 * math.log(ROPE_THETA) / MOBA_HEAD_DIM))
    ang = pos * inv
    cm_ref[...] = jnp.cos(ang)
    sn = jnp.sin(ang)
    sm_ref[...] = jnp.where(lane < half, -sn, sn)
    half = MLA_ROPE // 2
    inv = jnp.exp((lane % half).astype(F32) * (-2.0 * math.log(ROPE_THETA) / MLA_ROPE))
    ang = pos * inv
    valid = lane < MLA_ROPE
    cl_ref[...] = jnp.where(valid, jnp.cos(ang), 0.0)
    sn = jnp.sin(ang)
    sl_ref[...] = jnp.where(lane < half, -sn, jnp.where(valid, sn, 0.0))


def _rope_tables(pos_col):
    T = pos_col.shape[0]
    tm = 512
    out = jax.ShapeDtypeStruct((T, LANES), F32)
    spec = pl.BlockSpec((tm, LANES), lambda i: (i, 0))
    return pl.pallas_call(
        _rope_table_kernel,
        grid=(T // tm,),
        in_specs=[pl.BlockSpec((tm, 1), lambda i: (i, 0))],
        out_specs=[spec] * 4,
        out_shape=[out] * 4,
        compiler_params=_params("parallel"),
        name="rope_tables",
    )(pos_col)


def _rope_moba(x, cos, sin):
    return x * cos + pltpu.roll(x, MOBA_HEAD_DIM // 2, 1) * sin


def _rope_mla(x, cos, sin, lane):
    half = MLA_ROPE // 2
    partner = jnp.where(lane < half, pltpu.roll(x, LANES - half, 1), pltpu.roll(x, half, 1))
    return x * cos + partner * sin


C_COLS = MLA_Q_RANK + MLA_KV_RANK + LANES
MOBA_W = MOBA_HEADS * MOBA_HEAD_DIM


def _attn_prep_kernel(x_ref, g_ref, w_in_ref, gq_ref, wuq_ref, gkv_ref, wuk_ref, wuv_ref,
                      cm_ref, sm_ref, cl_ref, sl_ref,
                      ql_ref, kl_ref, vl_ref, qm_ref, km_ref, vm_ref, sel_ref,
                      kbar_ref, qmf_ref, kmf_ref, *, tm, tiles_per_seq, n_blocks):
    i = pl.program_id(0)

    @pl.when(i == 0)
    def _():
        kbar_ref[...] = jnp.zeros_like(kbar_ref)

    xn = _rms(x_ref[...], g_ref[...]).astype(BF16)
    lane = lax.broadcasted_iota(jnp.int32, (1, LANES), 1)
    cl, sl = cl_ref[...], sl_ref[...]
    cm, sm = cm_ref[...], sm_ref[...]

    c = _dot(xn, w_in_ref[:, 0:C_COLS])
    cq = _rms(c[:, 0:MLA_Q_RANK], gq_ref[...]).astype(BF16)
    q = _dot(cq, wuq_ref[...])
    ckv = _rms(c[:, MLA_Q_RANK:MLA_Q_RANK + MLA_KV_RANK], gkv_ref[...]).astype(BF16)
    kn = _dot(ckv, wuk_ref[...])
    vl_ref[...] = _dot(ckv, wuv_ref[...]).astype(BF16)
    kr = _rope_mla(c[:, MLA_Q_RANK + MLA_KV_RANK:C_COLS], cl, sl, lane).astype(BF16)
    for h in range(MLA_HEADS):
        o = h * MLA_QK_PAD
        ql_ref[:, o:o + LANES] = q[:, o:o + LANES].astype(BF16)
        ql_ref[:, o + LANES:o + 2 * LANES] = _rope_mla(q[:, o + LANES:o + 2 * LANES], cl, sl, lane).astype(BF16)
        kl_ref[:, o:o + LANES] = kn[:, h * LANES:(h + 1) * LANES].astype(BF16)
        kl_ref[:, o + LANES:o + 2 * LANES] = kr

    o0 = C_COLS
    qm = _dot(xn, w_in_ref[:, o0:o0 + MOBA_W])
    km = _dot(xn, w_in_ref[:, o0 + MOBA_W:o0 + 2 * MOBA_W])
    vm_ref[...] = _dot(xn, w_in_ref[:, o0 + 2 * MOBA_W:o0 + 3 * MOBA_W]).astype(BF16)
    for h in range(MOBA_HEADS):
        sl_h = slice(h * LANES, (h + 1) * LANES)
        qr = _rope_moba(qm[:, sl_h], cm, sm)
        kr_h = _rope_moba(km[:, sl_h], cm, sm)
        qmf_ref[:, sl_h] = qr
        kmf_ref[:, sl_h] = kr_h
        qm_ref[:, sl_h] = qr.astype(BF16)
        km_ref[:, sl_h] = kr_h.astype(BF16)

    blk0 = (i % tiles_per_seq) * (tm // MOBA_BLOCK)
    for s in range(tm // MOBA_BLOCK):
        kbar_ref[pl.ds(blk0 + s, 1), :] = jnp.mean(
            kmf_ref[s * MOBA_BLOCK:(s + 1) * MOBA_BLOCK, :], axis=0, keepdims=True)
    tok = lax.broadcasted_iota(jnp.int32, (n_blocks, tm), 1)
    own = blk0 + tok // MOBA_BLOCK
    blk = lax.broadcasted_iota(jnp.int32, (n_blocks, tm), 0)
    past = blk < own
    rows = []
    for h in range(MOBA_HEADS):
        sl_h = slice(h * LANES, (h + 1) * LANES)
        gate = _dot_nt(kbar_ref[:, sl_h], qmf_ref[:, sl_h], precision=lax.Precision.HIGHEST)
        gate = jnp.where(past, gate, NEG_INF)
        ahead = jnp.zeros((n_blocks, tm), jnp.int32)
        for m in range(n_blocks):
            gm = gate[m:m + 1, :]
            beats = (gm > gate) | ((gm == gate) & (m < blk))
            ahead = ahead + beats.astype(jnp.int32)
        rows.append(jnp.where(past & (ahead < MOBA_TOPK), 1.0, 0.0).astype(F32))
    pad = jnp.zeros((LANES - MOBA_HEADS * n_blocks, tm), F32)
    sel_ref[...] = jnp.concatenate(rows + [pad], axis=0).T


def _attn_prep(x2, g_attn, w_in_p, g_q, w_uq_p, g_kv, w_uk, w_uv, tabs, *, seq):
    T, D = x2.shape
    tm = 512
    n_blocks = seq // MOBA_BLOCK
    assert seq % tm == 0 and tm % MOBA_BLOCK == 0 and MOBA_HEADS * n_blocks <= LANES
    cm, sm, cl, sl = tabs
    row = lambda w: pl.BlockSpec((tm, w), lambda i: (i, 0))
    kern = functools.partial(_attn_prep_kernel, tm=tm, tiles_per_seq=seq // tm, n_blocks=n_blocks)
    outs = [(MLA_HEADS * MLA_QK_PAD, BF16), (MLA_HEADS * MLA_QK_PAD, BF16), (MLA_HEADS * MLA_V, BF16),
            (MOBA_W, BF16), (MOBA_W, BF16), (MOBA_W, BF16), (LANES, F32)]
    return pl.pallas_call(
        kern,
        grid=(T // tm,),
        in_specs=[row(D), _const_spec(g_attn.shape), _const_spec(w_in_p.shape), _const_spec(g_q.shape),
                  _const_spec(w_uq_p.shape), _const_spec(g_kv.shape), _const_spec(w_uk.shape),
                  _const_spec(w_uv.shape), row(LANES), row(LANES), row(LANES), row(LANES)],
        out_specs=[row(w) for w, _ in outs],
        out_shape=[jax.ShapeDtypeStruct((T, w), dt) for w, dt in outs],
        scratch_shapes=[pltpu.VMEM((n_blocks, MOBA_W), F32), pltpu.VMEM((tm, MOBA_W), F32),
                        pltpu.VMEM((tm, MOBA_W), F32)],
        compiler_params=_params("arbitrary"),
        name="attn_prep",
    )(x2, g_attn, w_in_p, g_q, w_uq_p, g_kv, w_uk, w_uv, cm, sm, cl, sl)


ATTN_TQ = 256


def _softmax_av(s, v):
    m = jnp.max(s, axis=-1, keepdims=True)
    e = jnp.exp(s - m)
    l = jnp.sum(e, axis=-1, keepdims=True)
    return _dot(e.astype(BF16), v) / l


def _causal_bias(tq):
    r = lax.broadcasted_iota(jnp.int32, (tq, tq), 0)
    c = lax.broadcasted_iota(jnp.int32, (tq, tq), 1)
    return jnp.where(c <= r, 0.0, NEG_INF).astype(F32)


def _pipelined_tiles(scores, attend, o_ref, n_tiles, tq):
    s_next = scores(0)
    for i in range(n_tiles):
        s = s_next
        if i + 1 < n_tiles:
            s_next = scores(i + 1)
        o_ref[i * tq:(i + 1) * tq, :] = attend(s, (i + 1) * tq)


def _mla_attn_kernel(q_ref, k_ref, v_ref, o_ref, *, seq, scale):
    tq = ATTN_TQ
    diag = _causal_bias(tq)

    def scores(i):
        n = (i + 1) * tq
        s = _dot_nt(q_ref[i * tq:n, :], k_ref[0:n, :]) * scale
        return s + jnp.concatenate([jnp.zeros((tq, i * tq), F32), diag], axis=1) if i else s + diag

    _pipelined_tiles(scores, lambda s, n: _softmax_av(s, v_ref[0:n, :]), o_ref, seq // tq, tq)


def _mla_attention(ql, kl, vl, *, batch, seq):
    T = ql.shape[0]
    scale = (MLA_NOPE + MLA_ROPE) ** -0.5
    return pl.pallas_call(
        functools.partial(_mla_attn_kernel, seq=seq, scale=scale),
        grid=(batch, MLA_HEADS),
        in_specs=[pl.BlockSpec((seq, MLA_QK_PAD), lambda b, h: (b, h)),
                  pl.BlockSpec((seq, MLA_QK_PAD), lambda b, h: (b, h)),
                  pl.BlockSpec((seq, MLA_V), lambda b, h: (b, h))],
        out_specs=pl.BlockSpec((seq, MLA_V), lambda b, h: (b, h)),
        out_shape=jax.ShapeDtypeStruct((T, MLA_HEADS * MLA_V), F32),
        compiler_params=_params("parallel", "parallel"),
        name="mla_attention",
    )(ql, kl, vl)


def _moba_attn_kernel(q_ref, k_ref, v_ref, sel_ref, o_ref, *, seq, scale, n_blocks):
    tq = ATTN_TQ
    h = pl.program_id(1)
    lane = lax.broadcasted_iota(jnp.int32, (1, LANES), 1)
    diag = _causal_bias(tq)

    def scores(i):
        n = (i + 1) * tq
        s = _dot_nt(q_ref[i * tq:n, :], k_ref[0:n, :]) * scale
        sel = sel_ref[i * tq:n, :]
        pieces = []
        for b in range(i):
            chosen = jnp.sum(jnp.where(lane == h * n_blocks + b, sel, 0.0), axis=-1, keepdims=True)
            pieces.append(jnp.broadcast_to(jnp.where(chosen > 0.0, 0.0, NEG_INF), (tq, tq)))
        return s + (jnp.concatenate(pieces + [diag], axis=1) if pieces else diag)

    _pipelined_tiles(scores, lambda s, n: _softmax_av(s, v_ref[0:n, :]), o_ref, seq // tq, tq)


def _moba_attention(qm, km, vm, sel, *, batch, seq):
    T = qm.shape[0]
    assert ATTN_TQ == MOBA_BLOCK
    hd = pl.BlockSpec((seq, MOBA_HEAD_DIM), lambda b, h: (b, h))
    return pl.pallas_call(
        functools.partial(_moba_attn_kernel, seq=seq, scale=MOBA_HEAD_DIM ** -0.5,
                          n_blocks=seq // MOBA_BLOCK),
        grid=(batch, MOBA_HEADS),
        in_specs=[hd, hd, hd, pl.BlockSpec((seq, LANES), lambda b, h: (b, 0))],
        out_specs=hd,
        out_shape=jax.ShapeDtypeStruct((T, MOBA_W), F32),
        compiler_params=_params("parallel", "parallel"),
        name="moba_attention",
    )(qm, km, vm, sel)


def _out_proj_kernel(x_ref, ol_ref, om_ref, gl_ref, gm_ref, w_ref, h_ref, *, wl):
    a = _rms(ol_ref[...], gl_ref[...]).astype(BF16)
    b = _rms(om_ref[...], gm_ref[...]).astype(BF16)
    h_ref[...] = x_ref[...] + (_dot(a, w_ref[0:wl, :]) + _dot(b, w_ref[wl:, :]))


def _out_proj(x2, o_mla, o_moba, g_l, g_m, w_out):
    T, D = x2.shape
    tm = 512
    row = lambda w: pl.BlockSpec((tm, w), lambda i: (i, 0))
    return pl.pallas_call(
        functools.partial(_out_proj_kernel, wl=o_mla.shape[1]),
        grid=(T // tm,),
        in_specs=[row(D), row(o_mla.shape[1]), row(o_moba.shape[1]), _const_spec(g_l.shape),
                  _const_spec(g_m.shape), _const_spec(w_out.shape)],
        out_specs=row(D),
        out_shape=jax.ShapeDtypeStruct((T, D), F32),
        compiler_params=_params("parallel"),
        name="out_proj",
    )(x2, o_mla, o_moba, g_l, g_m, w_out)


def _rms_matmul_kernel(x_ref, g_ref, w_ref, o_ref):
    xn = _rms(x_ref[...], g_ref[...]).astype(BF16)
    o_ref[...] = _dot(xn, w_ref[...]).astype(o_ref.dtype)


def _rms_matmul(x, g, w, out_dtype, *, tm, tn):
    M, K = x.shape
    N = w.shape[1]
    return pl.pallas_call(
        _rms_matmul_kernel,
        grid=(M // tm, N // tn),
        in_specs=[pl.BlockSpec((tm, K), lambda i, j: (i, 0)), pl.BlockSpec((1, K), lambda i, j: (0, 0)),
                  pl.BlockSpec((K, tn), lambda i, j: (0, j))],
        out_specs=pl.BlockSpec((tm, tn), lambda i, j: (i, j)),
        out_shape=jax.ShapeDtypeStruct((M, N), out_dtype),
        compiler_params=_params("parallel", "parallel"),
        name="rms_matmul",
    )(x, g, w)


def _cross_kernel(h_ref, g_ref, wq_ref, kv_ref, wo_ref, o_ref, *, scale):
    h = h_ref[...]
    D = h.shape[1]
    dh = D // CROSS_HEADS
    q = _dot(_rms(h, g_ref[...]).astype(BF16), wq_ref[...]).astype(BF16)
    outs = []
    for hd in range(CROSS_HEADS):
        s = _dot_nt(q[:, hd * dh:(hd + 1) * dh], kv_ref[:, hd * dh:(hd + 1) * dh]) * scale
        outs.append(_softmax_av(s, kv_ref[:, D + hd * dh:D + (hd + 1) * dh]).astype(BF16))
    o_ref[...] = h + _dot(jnp.concatenate(outs, axis=1), wo_ref[...])


def _cross_attention(h1, g_cross, w_cq, kv, w_co, *, seq, mem_len):
    T, D = h1.shape
    tm = 256
    tiles_per_seq = seq // tm
    row = pl.BlockSpec((tm, D), lambda i: (i, 0))
    return pl.pallas_call(
        functools.partial(_cross_kernel, scale=(D // CROSS_HEADS) ** -0.5),
        grid=(T // tm,),
        in_specs=[row, _const_spec(g_cross.shape), _const_spec(w_cq.shape),
                  pl.BlockSpec((mem_len, 2 * D), lambda i: (i // tiles_per_seq, 0)),
                  _const_spec(w_co.shape)],
        out_specs=row,
        out_shape=jax.ShapeDtypeStruct((T, D), F32),
        compiler_params=_params("parallel"),
        name="cross_attention",
    )(h1, g_cross, w_cq, kv, w_co)


def _top16_cols(s, exact):
    n, tm = s.shape
    idx = lax.broadcasted_iota(jnp.int32, (n, tm), 0).astype(F32) if exact else None
    rank = jnp.full((n, tm), float(PEER_TOPK), F32)
    vals = []
    for r in range(PEER_TOPK):
        m = jnp.max(s, axis=0, keepdims=True)
        hit = s == m
        if exact:
            hit = idx == jnp.min(jnp.where(hit, idx, float(n)), axis=0, keepdims=True)
        rank = jnp.where(hit, float(r), rank)
        s = jnp.where(hit, NEG_INF, s)
        vals.append(m)
    kept = jnp.sum(jnp.where(rank < float(PEER_TOPK), 1.0, 0.0), axis=0, keepdims=True)
    return vals, rank, kept == float(PEER_TOPK)


_PAIR_B = [PEER_TOPK // (a + 1) for a in range(8)]


def _pair_counts(v1, v2):
    tm = v1[0].shape[1]
    v1s = jnp.concatenate(v1, axis=0)
    v2s = jnp.concatenate(v2, axis=0)
    pieces, keys = [], []
    for a, nb in enumerate(_PAIR_B):
        rows = -(-nb // 8) * 8
        r = lax.broadcasted_iota(jnp.int32, (rows, tm), 0)
        pieces.append(jnp.where(r < nb, v2s[0:rows, :] + v1[a], NEG_INF))
        keys.append((a * PEER_TOPK + r).astype(F32))
    r = lax.broadcasted_iota(jnp.int32, (8, tm), 0)
    pieces.append(v1s[8:16, :] + v2[0])
    keys.append(((r + 8) * PEER_TOPK).astype(F32))
    cand = jnp.concatenate(pieces, axis=0)
    key = jnp.concatenate(keys, axis=0)
    work = cand
    for _ in range(PEER_TOPK):
        m = jnp.max(work, axis=0, keepdims=True)
        first = jnp.min(jnp.where(work == m, key, float(PEER_TOPK * PEER_TOPK)), axis=0, keepdims=True)
        work = jnp.where(key == first, NEG_INF, work)
    chosen = work != cand
    top = v1[0] + v2[0]
    z = jnp.sum(jnp.where(chosen, jnp.exp(cand - top), 0.0), axis=0, keepdims=True)
    picked = jnp.where(chosen, 1.0, 0.0)
    counts, o = [], 0
    for nb in _PAIR_B:
        rows = -(-nb // 8) * 8
        counts.append(jnp.sum(picked[o:o + rows, :], axis=0, keepdims=True))
        o += rows
    for a in range(8):
        counts.append(picked[o + a:o + a + 1, :])
    return counts, z


Z_E1, Z_CNT, Z_RANK2, Z_E2 = range(4)
Z_GROUPS = PEER_NKEYS // 8


def _peer_retrieve_unit(q_ref, k1_ref, k2_ref, z_ref, c, hd, exact):
    half = PEER_DKEY // 2
    hi = lax.Precision.HIGHEST
    ts = slice(c * LANES, (c + 1) * LANES)
    o = hd * PEER_DKEY
    s1 = _dot_nt(k1_ref[...], q_ref[ts, o:o + half], precision=hi)
    s2 = _dot_nt(k2_ref[...], q_ref[ts, o + half:o + 2 * half], precision=hi)
    v1, rank1, ok1 = _top16_cols(s1, exact)
    v2, rank2, ok2 = _top16_cols(s2, exact)
    counts, z = _pair_counts(v1, v2)
    cnt = jnp.zeros(s1.shape, F32)
    for a in range(PEER_TOPK):
        cnt = jnp.where(rank1 == float(a), counts[a], cnt)
    planes = {Z_E1: jnp.exp(s1 - v1[0]), Z_CNT: cnt, Z_RANK2: rank2, Z_E2: jnp.exp(s2 - v2[0]) / z}
    for p, val in planes.items():
        z_ref[c, hd, :, p] = val.reshape(Z_GROUPS, 8, LANES)
    return jnp.where(ok1 & ok2, 0.0, 1.0)


def _peer_select_kernel(h_ref, g_ref, wq_ref, k1_ref, k2_ref, xn_ref, z_ref, q_ref):
    xn = _rms(h_ref[...], g_ref[...]).astype(BF16)
    xn_ref[...] = xn
    q_ref[...] = _dot(xn, wq_ref[...])
    units = [(c, hd) for c in range(q_ref.shape[0] // LANES) for hd in range(PEER_HEADS)]
    redo = [jnp.max(_peer_retrieve_unit(q_ref, k1_ref, k2_ref, z_ref, c, hd, exact=False)) > 0.0
            for c, hd in units]
    for (c, hd), flag in zip(units, redo):
        @pl.when(flag)
        def _(c=c, hd=hd):
            _peer_retrieve_unit(q_ref, k1_ref, k2_ref, z_ref, c, hd, exact=True)


def _peer_select(h2, g_ffn, w_pq, k1, k2):
    T, D = h2.shape
    tm = 256
    row = pl.BlockSpec((tm, D), lambda i: (i, 0))
    z_block = (tm // LANES, PEER_HEADS, Z_GROUPS, 4, 8, LANES)
    return pl.pallas_call(
        _peer_select_kernel,
        grid=(T // tm,),
        in_specs=[row, _const_spec(g_ffn.shape), _const_spec(w_pq.shape), _const_spec(k1.shape),
                  _const_spec(k2.shape)],
        out_specs=[row, pl.BlockSpec(z_block, lambda i: (i, 0, 0, 0, 0, 0))],
        out_shape=[jax.ShapeDtypeStruct((T, D), BF16),
                   jax.ShapeDtypeStruct((T // LANES,) + z_block[1:], F32)],
        scratch_shapes=[pltpu.VMEM((tm, w_pq.shape[1]), F32)],
        compiler_params=_params("parallel"),
        name="peer_select",
    )(h2, g_ffn, w_pq, k1, k2)


GATE_ROWS = 32
MXU_COLS = 256


def _peer_gate_block(at_ref, wt_ref, z_ref, group, sub0, c, sg, *, te):
    n_i1 = te // PEER_NKEYS
    groups_per_block = GATE_ROWS // 8
    gs = slice(sg * groups_per_block, (sg + 1) * groups_per_block)
    gate = [None] * n_i1
    for hd in range(PEER_HEADS):
        rank2 = z_ref[c, hd, gs, Z_RANK2]
        e2 = z_ref[c, hd, gs, Z_E2]
        for r in range(n_i1):
            rs = slice(sub0 + r, sub0 + r + 1)
            e1 = z_ref[c, hd, group, Z_E1, rs, :]
            cnt = z_ref[c, hd, group, Z_CNT, rs, :]
            term = e1 * jnp.where(rank2 < cnt, e2, 0.0)
            gate[r] = term if gate[r] is None else gate[r] + term
    for r in range(n_i1):
        es = slice(r * PEER_NKEYS + sg * GATE_ROWS, r * PEER_NKEYS + (sg + 1) * GATE_ROWS)
        a = at_ref[c, es, :]
        act = 0.5 * a * (1.0 + lax.erf(a * math.sqrt(0.5)))
        wt_ref[c, es, :] = (gate[r].reshape(GATE_ROWS, LANES) * act).astype(BF16)


def _peer_dense_kernel(xn_ref, u_ref, vt_ref, z_ref, h_ref, gf_ref,
                       o_ref, acc_ref, at0_ref, at1_ref, wt0_ref, wt1_ref, *, te, tm, n_tiles):
    j = pl.program_id(1)

    @pl.when(j == 0)
    def _():
        acc_ref[...] = jnp.zeros_like(acc_ref)

    rows_per_step = te // PEER_NKEYS
    assert 2 * rows_per_step == 8 and n_tiles % 2 == 0

    n_chunks = tm // LANES
    assert n_chunks == 4 and 2 * LANES == MXU_COLS
    n_sg = PEER_NKEYS // GATE_ROWS
    d_piece = acc_ref.shape[1] // (n_chunks * n_sg // 2)

    def step(parity, do_a=True, do_b=True, do_c=True):
        at_w, at_r = (at0_ref, at1_ref) if parity == 0 else (at1_ref, at0_ref)
        wt_w, wt_r = (wt1_ref, wt0_ref) if parity == 0 else (wt0_ref, wt1_ref)
        group, sub0 = (j - 1) // 2, (rows_per_step if parity == 0 else 0)
        for c in range(n_chunks):
            for sg in range(n_sg):
                k = c * n_sg + sg
                p, dr = k % 2, slice((k // 2) * d_piece, (k // 2 + 1) * d_piece)
                if do_c:
                    w = jnp.concatenate([wt_r[2 * p], wt_r[2 * p + 1]], axis=1)
                    res = _dot(vt_ref[dr, :], w)
                    acc_ref[2 * p, dr, :] += res[:, :LANES]
                    acc_ref[2 * p + 1, dr, :] += res[:, LANES:]
                if do_b:
                    _peer_gate_block(at_r, wt_w, z_ref, group, sub0, c, sg, te=te)
            if do_a:
                c0, er = 2 * (c // 2), slice((c % 2) * (te // 2), (c % 2 + 1) * (te // 2))
                res = _dot_nt(u_ref[er, :], xn_ref[c0 * LANES:(c0 + 2) * LANES, :])
                at_w[c0, er, :] = res[:, :LANES]
                at_w[c0 + 1, er, :] = res[:, LANES:]

    last = n_tiles + 1
    pl.when(j == 0)(lambda: step(0, do_b=False, do_c=False))
    pl.when(j == 1)(lambda: step(1, do_c=False))
    pl.when((j % 2 == 0) & (j >= 2) & (j < n_tiles))(lambda: step(0))
    pl.when((j % 2 == 1) & (j >= 2) & (j < n_tiles))(lambda: step(1))
    pl.when(j == n_tiles)(lambda: step(0, do_a=False))
    pl.when(j == last)(lambda: step(1, do_a=False, do_b=False))

    @pl.when(j == last)
    def _():
        for c in range(n_chunks):
            rows = slice(c * LANES, (c + 1) * LANES)
            o_ref[rows, :] = _rms(h_ref[rows, :] + acc_ref[c].T, gf_ref[...])


PEER_TE = 512


def _peer_dense(xn, u, vt, z, h2, g_final):
    T, D = h2.shape
    tm, te = 512, PEER_TE
    n_tiles = vt.shape[0]
    n_chunks = tm // LANES
    clamp = lambda t: jnp.minimum(jnp.maximum(t, 0), n_tiles - 1)
    row = pl.BlockSpec((tm, D), lambda i, j: (i, 0))
    return pl.pallas_call(
        functools.partial(_peer_dense_kernel, te=te, tm=tm, n_tiles=n_tiles),
        grid=(T // tm, n_tiles + 2),
        in_specs=[row, pl.BlockSpec((te, D), lambda i, j: (clamp(j), 0)),
                  pl.BlockSpec((None, D, te), lambda i, j: (clamp(j - 2), 0, 0)),
                  pl.BlockSpec((n_chunks,) + z.shape[1:], lambda i, j: (i, 0, 0, 0, 0, 0)),
                  row, pl.BlockSpec((1, D), lambda i, j: (0, 0))],
        out_specs=row,
        out_shape=jax.ShapeDtypeStruct((T, D), F32),
        scratch_shapes=[pltpu.VMEM((n_chunks, D, LANES), F32),
                        pltpu.VMEM((n_chunks, te, LANES), F32), pltpu.VMEM((n_chunks, te, LANES), F32),
                        pltpu.VMEM((n_chunks, te, LANES), BF16), pltpu.VMEM((n_chunks, te, LANES), BF16)],
        compiler_params=_params("parallel", "arbitrary"),
        name="peer_dense",
    )(xn, u, vt, z, h2, g_final)


def _layer(h, mem2, tabs, g_attn, w_in, g_q_a, w_uq, g_kv_a, w_ukv, g_mla_o, g_moba_o, w_out, g_cross,
           g_mem, w_cq, w_ck, w_cv, w_co, g_ffn, w_pq, sub_keys1, sub_keys2, expert_u, expert_v,
           g_final, *, batch, seq, mem_len):
    D = h.shape[1]
    row = lambda g: g.reshape(1, -1)
    n_lat = MLA_Q_RANK + MLA_KV_RANK + MLA_ROPE
    w_in_p = jnp.concatenate([w_in[:, :n_lat], jnp.zeros((D, LANES - MLA_ROPE), w_in.dtype),
                              w_in[:, n_lat:]], axis=1).astype(BF16)
    w_uq_p = jnp.pad(w_uq.reshape(MLA_Q_RANK, MLA_HEADS, MLA_NOPE + MLA_ROPE),
                     ((0, 0), (0, 0), (0, MLA_QK_PAD - MLA_NOPE - MLA_ROPE))
                     ).reshape(MLA_Q_RANK, MLA_HEADS * MLA_QK_PAD).astype(BF16)
    w_ukv3 = w_ukv.reshape(MLA_KV_RANK, MLA_HEADS, MLA_NOPE + MLA_V)
    w_uk = w_ukv3[:, :, :MLA_NOPE].reshape(MLA_KV_RANK, MLA_HEADS * MLA_NOPE).astype(BF16)
    w_uv = w_ukv3[:, :, MLA_NOPE:].reshape(MLA_KV_RANK, MLA_HEADS * MLA_V).astype(BF16)

    ql, kl, vl, qm, km, vm, sel = _attn_prep(h, row(g_attn), w_in_p, row(g_q_a), w_uq_p, row(g_kv_a),
                                             w_uk, w_uv, tabs, seq=seq)
    o_mla = _mla_attention(ql, kl, vl, batch=batch, seq=seq)
    o_moba = _moba_attention(qm, km, vm, sel, batch=batch, seq=seq)
    h1 = _out_proj(h, o_mla, o_moba, row(g_mla_o), row(g_moba_o), w_out.astype(BF16))

    w_ckv = jnp.concatenate([w_ck, w_cv], axis=1).astype(BF16)
    kv = _rms_matmul(mem2, row(g_mem), w_ckv, BF16, tm=min(mem2.shape[0], 512), tn=1024)
    h2 = _cross_attention(h1, row(g_cross), w_cq.astype(BF16), kv, w_co.astype(BF16), seq=seq, mem_len=mem_len)

    xn, z = _peer_select(h2, row(g_ffn), w_pq.astype(BF16), sub_keys1, sub_keys2)
    vt = expert_v.reshape(-1, PEER_TE, D).transpose(0, 2, 1).astype(BF16)
    return _peer_dense(xn, expert_u.astype(BF16), vt, z, h2, row(g_final))


def kernel(x, mem, positions, g_attn, w_in, g_q_a, w_uq, g_kv_a, w_ukv, g_mla_o, g_moba_o, w_out, g_cross,
           g_mem, w_cq, w_ck, w_cv, w_co, g_ffn, w_pq, sub_keys1, sub_keys2, expert_u, expert_v, g_final):
    B, S, D = x.shape
    M = mem.shape[1]
    depth = w_in.shape[0]
    assert depth == 1, "the final norm is fused into the layer's last kernel"
    tabs = _rope_tables(positions.reshape(B * S, 1))
    out = _layer(x.reshape(B * S, D), mem.reshape(B * M, D), tabs, g_attn[0], w_in[0], g_q_a[0], w_uq[0],
                 g_kv_a[0], w_ukv[0], g_mla_o[0], g_moba_o[0], w_out[0], g_cross[0], g_mem[0], w_cq[0],
                 w_ck[0], w_cv[0], w_co[0], g_ffn[0], w_pq[0], sub_keys1[0], sub_keys2[0], expert_u[0],
                 expert_v[0], g_final, batch=B, seq=S, mem_len=M)
    return out.reshape(B, S, D)
```

```python
import functools
import math

import jax
import jax.numpy as jnp
from jax import lax
from jax.experimental import pallas as pl
from jax.experimental.pallas import tpu as pltpu

F32 = jnp.float32
BF16 = jnp.bfloat16

EPS = 1e-6
ROPE_THETA = 10000.0
MLA_HEADS = 8
MLA_Q_RANK = 512
MLA_KV_RANK = 256
MLA_NOPE = 128
MLA_ROPE = 64
MLA_V = 128
MLA_QK_PAD = 256
MOBA_HEADS = 8
MOBA_HEAD_DIM = 128
MOBA_BLOCK = 256
MOBA_TOPK = 3
CROSS_HEADS = 4
PEER_HEADS = 8
PEER_NKEYS = 128
PEER_DKEY = 256
PEER_TOPK = 16

LANES = 128
VMEM_LIMIT = 56 * 1024 * 1024

NEG_INF = float("-inf")
NT_DIMS = (((1,), (1,)), ((), ()))


def _rms(x, g):
    ms = jnp.mean(x * x, axis=-1, keepdims=True)
    return x * lax.rsqrt(ms + EPS) * g


def _dot(a, b):
    return jnp.dot(a, b, preferred_element_type=F32)


def _dot_nt(a, b, precision=None):
    return lax.dot_general(a, b, NT_DIMS, precision=precision, preferred_element_type=F32)


def _const_spec(shape):
    nd = len(shape)
    return pl.BlockSpec(shape, lambda *_: (0,) * nd, pipeline_mode=pl.Buffered(1))


def _params(*sem):
    return pltpu.CompilerParams(dimension_semantics=sem, vmem_limit_bytes=VMEM_LIMIT)


def _rope_table_kernel(pos_ref, cm_ref, sm_ref, cl_ref, sl_ref):
    pos = pos_ref[...].astype(F32)
    lane = lax.broadcasted_iota(jnp.int32, (1, LANES), 1)
    hm, hl = MOBA_HEAD_DIM // 2, MLA_ROPE // 2
    inv = jnp.where(lane < hm,
                    jnp.exp(lane.astype(F32) * (-2.0 * math.log(ROPE_THETA) / MOBA_HEAD_DIM)),
                    jnp.exp((lane % hl).astype(F32) * (-2.0 * math.log(ROPE_THETA) / MLA_ROPE)))
    ang = pos * inv
    cs, sn = jnp.cos(ang), jnp.sin(ang)
    cm_ref[...] = jnp.where(lane < hm, cs, pltpu.roll(cs, hm, 1))
    sm_ref[...] = jnp.where(lane < hm, -sn, pltpu.roll(sn, hm, 1))
    lo, hi = lane < hl, lane < MLA_ROPE
    cl_ref[...] = jnp.where(lo, pltpu.roll(cs, hm, 1), jnp.where(hi, pltpu.roll(cs, hm + hl, 1), 0.0))
    sl_ref[...] = jnp.where(lo, -pltpu.roll(sn, hm, 1), jnp.where(hi, pltpu.roll(sn, hm + hl, 1), 0.0))


def _rope_tables(pos_col):
    T = pos_col.shape[0]
    tm = 512
    out = jax.ShapeDtypeStruct((T, LANES), F32)
    spec = pl.BlockSpec((tm, LANES), lambda i: (i, 0))
    return pl.pallas_call(
        _rope_table_kernel,
        grid=(T // tm,),
        in_specs=[pl.BlockSpec((tm, 1), lambda i: (i, 0))],
        out_specs=[spec] * 4,
        out_shape=[out] * 4,
        compiler_params=_params("parallel"),
        name="rope_tables",
    )(pos_col)


def _rope_moba(x, cos, sin):
    return x * cos + pltpu.roll(x, MOBA_HEAD_DIM // 2, 1) * sin


def _rope_mla(x, cos, sin, lane):
    half = MLA_ROPE // 2
    partner = jnp.where(lane < half, pltpu.roll(x, LANES - half, 1), pltpu.roll(x, half, 1))
    return x * cos + partner * sin


C_COLS = MLA_Q_RANK + MLA_KV_RANK + LANES
MOBA_W = MOBA_HEADS * MOBA_HEAD_DIM


def _attn_prep_kernel(x_ref, g_ref, w_in_ref, gq_ref, wuq_ref, gkv_ref, wuk_ref, wuv_ref,
                      cm_ref, sm_ref, cl_ref, sl_ref,
                      ql_ref, kl_ref, vl_ref, qm_ref, km_ref, vm_ref, sel_ref,
                      kbar_ref, qmf_ref, kmf_ref, *, tm, tiles_per_seq, n_blocks):
    i = pl.program_id(0)

    @pl.when(i == 0)
    def _():
        kbar_ref[...] = jnp.zeros_like(kbar_ref)

    xn = _rms(x_ref[...], g_ref[...]).astype(BF16)
    lane = lax.broadcasted_iota(jnp.int32, (1, LANES), 1)
    cl, sl = cl_ref[...], sl_ref[...]
    cm, sm = cm_ref[...], sm_ref[...]

    c = _dot(xn, w_in_ref[:, 0:C_COLS])
    cq = _rms(c[:, 0:MLA_Q_RANK], gq_ref[...]).astype(BF16)
    q = _dot(cq, wuq_ref[...])
    ckv = _rms(c[:, MLA_Q_RANK:MLA_Q_RANK + MLA_KV_RANK], gkv_ref[...]).astype(BF16)
    kn = _dot(ckv, wuk_ref[...])
    vl_ref[...] = _dot(ckv, wuv_ref[...]).astype(BF16)
    kr = _rope_mla(c[:, MLA_Q_RANK + MLA_KV_RANK:C_COLS], cl, sl, lane).astype(BF16)
    for h in range(MLA_HEADS):
        o = h * MLA_QK_PAD
        ql_ref[:, o:o + LANES] = q[:, o:o + LANES].astype(BF16)
        ql_ref[:, o + LANES:o + 2 * LANES] = _rope_mla(q[:, o + LANES:o + 2 * LANES], cl, sl, lane).astype(BF16)
        kl_ref[:, o:o + LANES] = kn[:, h * LANES:(h + 1) * LANES].astype(BF16)
        kl_ref[:, o + LANES:o + 2 * LANES] = kr

    o0 = C_COLS
    qm = _dot(xn, w_in_ref[:, o0:o0 + MOBA_W])
    km = _dot(xn, w_in_ref[:, o0 + MOBA_W:o0 + 2 * MOBA_W])
    vm_ref[...] = _dot(xn, w_in_ref[:, o0 + 2 * MOBA_W:o0 + 3 * MOBA_W]).astype(BF16)
    for h in range(MOBA_HEADS):
        sl_h = slice(h * LANES, (h + 1) * LANES)
        qr = _rope_moba(qm[:, sl_h], cm, sm)
        kr_h = _rope_moba(km[:, sl_h], cm, sm)
        qmf_ref[:, sl_h] = qr
        kmf_ref[:, sl_h] = kr_h
        qm_ref[:, sl_h] = qr.astype(BF16)
        km_ref[:, sl_h] = kr_h.astype(BF16)

    blk0 = (i % tiles_per_seq) * (tm // MOBA_BLOCK)
    for s in range(tm // MOBA_BLOCK):
        kbar_ref[pl.ds(blk0 + s, 1), :] = jnp.mean(
            kmf_ref[s * MOBA_BLOCK:(s + 1) * MOBA_BLOCK, :], axis=0, keepdims=True)
    tok = lax.broadcasted_iota(jnp.int32, (n_blocks, tm), 1)
    own = blk0 + tok // MOBA_BLOCK
    blk = lax.broadcasted_iota(jnp.int32, (n_blocks, tm), 0)
    past = blk < own
    rows = []
    for h in range(MOBA_HEADS):
        sl_h = slice(h * LANES, (h + 1) * LANES)
        gate = _dot_nt(kbar_ref[:, sl_h], qmf_ref[:, sl_h], precision=lax.Precision.HIGHEST)
        gate = jnp.where(past, gate, NEG_INF)
        ahead = jnp.zeros((n_blocks, tm), jnp.int32)
        for m in range(n_blocks):
            gm = gate[m:m + 1, :]
            beats = (gm > gate) | ((gm == gate) & (m < blk))
            ahead = ahead + beats.astype(jnp.int32)
        rows.append(jnp.where(past & (ahead < MOBA_TOPK), 1.0, 0.0).astype(F32))
    pad = jnp.zeros((LANES - MOBA_HEADS * n_blocks, tm), F32)
    sel_ref[...] = jnp.concatenate(rows + [pad], axis=0).T


def _attn_prep(x2, g_attn, w_in_p, g_q, w_uq_p, g_kv, w_uk, w_uv, tabs, *, seq):
    T, D = x2.shape
    tm = 512
    n_blocks = seq // MOBA_BLOCK
    assert seq % tm == 0 and tm % MOBA_BLOCK == 0 and MOBA_HEADS * n_blocks <= LANES
    cm, sm, cl, sl = tabs
    row = lambda w: pl.BlockSpec((tm, w), lambda i: (i, 0))
    kern = functools.partial(_attn_prep_kernel, tm=tm, tiles_per_seq=seq // tm, n_blocks=n_blocks)
    outs = [(MLA_HEADS * MLA_QK_PAD, BF16), (MLA_HEADS * MLA_QK_PAD, BF16), (MLA_HEADS * MLA_V, BF16),
            (MOBA_W, BF16), (MOBA_W, BF16), (MOBA_W, BF16), (LANES, F32)]
    return pl.pallas_call(
        kern,
        grid=(T // tm,),
        in_specs=[row(D), _const_spec(g_attn.shape), _const_spec(w_in_p.shape), _const_spec(g_q.shape),
                  _const_spec(w_uq_p.shape), _const_spec(g_kv.shape), _const_spec(w_uk.shape),
                  _const_spec(w_uv.shape), row(LANES), row(LANES), row(LANES), row(LANES)],
        out_specs=[row(w) for w, _ in outs],
        out_shape=[jax.ShapeDtypeStruct((T, w), dt) for w, dt in outs],
        scratch_shapes=[pltpu.VMEM((n_blocks, MOBA_W), F32), pltpu.VMEM((tm, MOBA_W), F32),
                        pltpu.VMEM((tm, MOBA_W), F32)],
        compiler_params=_params("arbitrary"),
        name="attn_prep",
    )(x2, g_attn, w_in_p, g_q, w_uq_p, g_kv, w_uk, w_uv, cm, sm, cl, sl)


ATTN_TQ = 256


def _softmax_av(s, v):
    m = jnp.max(s, axis=-1, keepdims=True)
    e = jnp.exp(s - m)
    l = jnp.sum(e, axis=-1, keepdims=True)
    return _dot(e.astype(BF16), v) / l


def _causal_bias(tq):
    r = lax.broadcasted_iota(jnp.int32, (tq, tq), 0)
    c = lax.broadcasted_iota(jnp.int32, (tq, tq), 1)
    return jnp.where(c <= r, 0.0, NEG_INF).astype(F32)


def _pipelined_tiles(scores, attend, o_ref, n_tiles, tq):
    s_next = scores(0)
    for i in range(n_tiles):
        s = s_next
        if i + 1 < n_tiles:
            s_next = scores(i + 1)
        o_ref[i * tq:(i + 1) * tq, :] = attend(s, (i + 1) * tq)


def _mla_attn_kernel(q_ref, k_ref, v_ref, o_ref, *, seq, scale):
    tq = ATTN_TQ
    diag = _causal_bias(tq)

    def scores(i):
        n = (i + 1) * tq
        s = _dot_nt(q_ref[i * tq:n, :], k_ref[0:n, :]) * scale
        return s + jnp.concatenate([jnp.zeros((tq, i * tq), F32), diag], axis=1) if i else s + diag

    _pipelined_tiles(scores, lambda s, n: _softmax_av(s, v_ref[0:n, :]), o_ref, seq // tq, tq)


def _mla_attention(ql, kl, vl, *, batch, seq):
    T = ql.shape[0]
    scale = (MLA_NOPE + MLA_ROPE) ** -0.5
    return pl.pallas_call(
        functools.partial(_mla_attn_kernel, seq=seq, scale=scale),
        grid=(batch, MLA_HEADS),
        in_specs=[pl.BlockSpec((seq, MLA_QK_PAD), lambda b, h: (b, h)),
                  pl.BlockSpec((seq, MLA_QK_PAD), lambda b, h: (b, h)),
                  pl.BlockSpec((seq, MLA_V), lambda b, h: (b, h))],
        out_specs=pl.BlockSpec((seq, MLA_V), lambda b, h: (b, h)),
        out_shape=jax.ShapeDtypeStruct((T, MLA_HEADS * MLA_V), F32),
        compiler_params=_params("parallel", "parallel"),
        name="mla_attention",
    )(ql, kl, vl)


def _moba_attn_kernel(q_ref, k_ref, v_ref, sel_ref, o_ref, *, seq, scale, n_blocks):
    tq = ATTN_TQ
    h = pl.program_id(1)
    lane = lax.broadcasted_iota(jnp.int32, (1, LANES), 1)
    diag = _causal_bias(tq)

    def scores(i):
        n = (i + 1) * tq
        s = _dot_nt(q_ref[i * tq:n, :], k_ref[0:n, :]) * scale
        sel = sel_ref[i * tq:n, :]
        pieces = []
        for b in range(i):
            chosen = jnp.sum(jnp.where(lane == h * n_blocks + b, sel, 0.0), axis=-1, keepdims=True)
            pieces.append(jnp.broadcast_to(jnp.where(chosen > 0.0, 0.0, NEG_INF), (tq, tq)))
        return s + (jnp.concatenate(pieces + [diag], axis=1) if pieces else diag)

    _pipelined_tiles(scores, lambda s, n: _softmax_av(s, v_ref[0:n, :]), o_ref, seq // tq, tq)


def _moba_attention(qm, km, vm, sel, *, batch, seq):
    T = qm.shape[0]
    assert ATTN_TQ == MOBA_BLOCK
    hd = pl.BlockSpec((seq, MOBA_HEAD_DIM), lambda b, h: (b, h))
    return pl.pallas_call(
        functools.partial(_moba_attn_kernel, seq=seq, scale=MOBA_HEAD_DIM ** -0.5,
                          n_blocks=seq // MOBA_BLOCK),
        grid=(batch, MOBA_HEADS),
        in_specs=[hd, hd, hd, pl.BlockSpec((seq, LANES), lambda b, h: (b, 0))],
        out_specs=hd,
        out_shape=jax.ShapeDtypeStruct((T, MOBA_W), F32),
        compiler_params=_params("parallel", "parallel"),
        name="moba_attention",
    )(qm, km, vm, sel)


def _out_proj_kernel(x_ref, ol_ref, om_ref, gl_ref, gm_ref, w_ref, h_ref, *, wl):
    a = _rms(ol_ref[...], gl_ref[...]).astype(BF16)
    b = _rms(om_ref[...], gm_ref[...]).astype(BF16)
    h_ref[...] = x_ref[...] + (_dot(a, w_ref[0:wl, :]) + _dot(b, w_ref[wl:, :]))


def _out_proj(x2, o_mla, o_moba, g_l, g_m, w_out):
    T, D = x2.shape
    tm = 512
    row = lambda w: pl.BlockSpec((tm, w), lambda i: (i, 0))
    return pl.pallas_call(
        functools.partial(_out_proj_kernel, wl=o_mla.shape[1]),
        grid=(T // tm,),
        in_specs=[row(D), row(o_mla.shape[1]), row(o_moba.shape[1]), _const_spec(g_l.shape),
                  _const_spec(g_m.shape), _const_spec(w_out.shape)],
        out_specs=row(D),
        out_shape=jax.ShapeDtypeStruct((T, D), F32),
        compiler_params=_params("parallel"),
        name="out_proj",
    )(x2, o_mla, o_moba, g_l, g_m, w_out)


def _rms_matmul_kernel(x_ref, g_ref, w_ref, o_ref):
    xn = _rms(x_ref[...], g_ref[...]).astype(BF16)
    o_ref[...] = _dot(xn, w_ref[...]).astype(o_ref.dtype)


def _rms_matmul(x, g, w, out_dtype, *, tm, tn):
    M, K = x.shape
    N = w.shape[1]
    return pl.pallas_call(
        _rms_matmul_kernel,
        grid=(M // tm, N // tn),
        in_specs=[pl.BlockSpec((tm, K), lambda i, j: (i, 0)), pl.BlockSpec((1, K), lambda i, j: (0, 0)),
                  pl.BlockSpec((K, tn), lambda i, j: (0, j))],
        out_specs=pl.BlockSpec((tm, tn), lambda i, j: (i, j)),
        out_shape=jax.ShapeDtypeStruct((M, N), out_dtype),
        compiler_params=_params("parallel", "parallel"),
        name="rms_matmul",
    )(x, g, w)


def _cross_kernel(h_ref, g_ref, wq_ref, kv_ref, wo_ref, o_ref, *, scale):
    h = h_ref[...]
    D = h.shape[1]
    dh = D // CROSS_HEADS
    q = _dot(_rms(h, g_ref[...]).astype(BF16), wq_ref[...]).astype(BF16)
    outs = []
    for hd in range(CROSS_HEADS):
        s = _dot_nt(q[:, hd * dh:(hd + 1) * dh], kv_ref[:, hd * dh:(hd + 1) * dh]) * scale
        outs.append(_softmax_av(s, kv_ref[:, D + hd * dh:D + (hd + 1) * dh]).astype(BF16))
    o_ref[...] = h + _dot(jnp.concatenate(outs, axis=1), wo_ref[...])


def _cross_attention(h1, g_cross, w_cq, kv, w_co, *, seq, mem_len):
    T, D = h1.shape
    tm = 256
    tiles_per_seq = seq // tm
    row = pl.BlockSpec((tm, D), lambda i: (i, 0))
    return pl.pallas_call(
        functools.partial(_cross_kernel, scale=(D // CROSS_HEADS) ** -0.5),
        grid=(T // tm,),
        in_specs=[row, _const_spec(g_cross.shape), _const_spec(w_cq.shape),
                  pl.BlockSpec((mem_len, 2 * D), lambda i: (i // tiles_per_seq, 0)),
                  _const_spec(w_co.shape)],
        out_specs=row,
        out_shape=jax.ShapeDtypeStruct((T, D), F32),
        compiler_params=_params("parallel"),
        name="cross_attention",
    )(h1, g_cross, w_cq, kv, w_co)


def _top16_cols(s, exact):
    n, tm = s.shape
    idx = lax.broadcasted_iota(jnp.int32, (n, tm), 0).astype(F32) if exact else None
    rank = jnp.full((n, tm), float(PEER_TOPK), F32)
    vals = []
    for r in range(PEER_TOPK):
        m = jnp.max(s, axis=0, keepdims=True)
        hit = s == m
        if exact:
            hit = idx == jnp.min(jnp.where(hit, idx, float(n)), axis=0, keepdims=True)
        rank = jnp.where(hit, float(r), rank)
        s = jnp.where(hit, NEG_INF, s)
        vals.append(m)
    kept = jnp.sum(jnp.where(rank < float(PEER_TOPK), 1.0, 0.0), axis=0, keepdims=True)
    return vals, rank, kept == float(PEER_TOPK)


_PAIR_B = [PEER_TOPK // (a + 1) for a in range(8)]


def _pair_counts(v1, v2):
    tm = v1[0].shape[1]
    v1s = jnp.concatenate(v1, axis=0)
    v2s = jnp.concatenate(v2, axis=0)
    pieces, keys = [], []
    for a, nb in enumerate(_PAIR_B):
        rows = -(-nb // 8) * 8
        r = lax.broadcasted_iota(jnp.int32, (rows, tm), 0)
        pieces.append(jnp.where(r < nb, v2s[0:rows, :] + v1[a], NEG_INF))
        keys.append((a * PEER_TOPK + r).astype(F32))
    r = lax.broadcasted_iota(jnp.int32, (8, tm), 0)
    pieces.append(v1s[8:16, :] + v2[0])
    keys.append(((r + 8) * PEER_TOPK).astype(F32))
    cand = jnp.concatenate(pieces, axis=0)
    key = jnp.concatenate(keys, axis=0)
    work = cand
    for _ in range(PEER_TOPK):
        m = jnp.max(work, axis=0, keepdims=True)
        first = jnp.min(jnp.where(work == m, key, float(PEER_TOPK * PEER_TOPK)), axis=0, keepdims=True)
        work = jnp.where(key == first, NEG_INF, work)
    chosen = work != cand
    top = v1[0] + v2[0]
    z = jnp.sum(jnp.where(chosen, jnp.exp(cand - top), 0.0), axis=0, keepdims=True)
    picked = jnp.where(chosen, 1.0, 0.0)
    counts, o = [], 0
    for nb in _PAIR_B:
        rows = -(-nb // 8) * 8
        counts.append(jnp.sum(picked[o:o + rows, :], axis=0, keepdims=True))
        o += rows
    for a in range(8):
        counts.append(picked[o + a:o + a + 1, :])
    return counts, z


Z_E1, Z_CNT, Z_RANK2, Z_E2 = range(4)
Z_GROUPS = PEER_NKEYS // 8


def _peer_retrieve_unit(q_ref, k1_ref, k2_ref, z_ref, c, hd, exact):
    half = PEER_DKEY // 2
    hi = lax.Precision.HIGHEST
    ts = slice(c * LANES, (c + 1) * LANES)
    o = hd * PEER_DKEY
    s1 = _dot_nt(k1_ref[...], q_ref[ts, o:o + half], precision=hi)
    s2 = _dot_nt(k2_ref[...], q_ref[ts, o + half:o + 2 * half], precision=hi)
    v1, rank1, ok1 = _top16_cols(s1, exact)
    v2, rank2, ok2 = _top16_cols(s2, exact)
    counts, z = _pair_counts(v1, v2)
    cnt = jnp.zeros(s1.shape, F32)
    for a in range(PEER_TOPK):
        cnt = jnp.where(rank1 == float(a), counts[a], cnt)
    planes = {Z_E1: jnp.exp(s1 - v1[0]), Z_CNT: cnt, Z_RANK2: rank2, Z_E2: jnp.exp(s2 - v2[0]) / z * 0.5}
    for p, val in planes.items():
        z_ref[c, hd, :, p] = val.reshape(Z_GROUPS, 8, LANES)
    return jnp.where(ok1 & ok2, 0.0, 1.0)


def _peer_select_kernel(h_ref, g_ref, wq_ref, k1_ref, k2_ref, xn_ref, z_ref, q_ref):
    xn = _rms(h_ref[...], g_ref[...]).astype(BF16)
    xn_ref[...] = xn
    q_ref[...] = _dot(xn, wq_ref[...])
    units = [(c, hd) for c in range(q_ref.shape[0] // LANES) for hd in range(PEER_HEADS)]
    redo = [jnp.max(_peer_retrieve_unit(q_ref, k1_ref, k2_ref, z_ref, c, hd, exact=False)) > 0.0
            for c, hd in units]
    for (c, hd), flag in zip(units, redo):
        @pl.when(flag)
        def _(c=c, hd=hd):
            _peer_retrieve_unit(q_ref, k1_ref, k2_ref, z_ref, c, hd, exact=True)


def _peer_select(h2, g_ffn, w_pq, k1, k2):
    T, D = h2.shape
    tm = 256
    row = pl.BlockSpec((tm, D), lambda i: (i, 0))
    z_block = (tm // LANES, PEER_HEADS, Z_GROUPS, 4, 8, LANES)
    return pl.pallas_call(
        _peer_select_kernel,
        grid=(T // tm,),
        in_specs=[row, _const_spec(g_ffn.shape), _const_spec(w_pq.shape), _const_spec(k1.shape),
                  _const_spec(k2.shape)],
        out_specs=[row, pl.BlockSpec(z_block, lambda i: (i, 0, 0, 0, 0, 0))],
        out_shape=[jax.ShapeDtypeStruct((T, D), BF16),
                   jax.ShapeDtypeStruct((T // LANES,) + z_block[1:], F32)],
        scratch_shapes=[pltpu.VMEM((tm, w_pq.shape[1]), F32)],
        compiler_params=_params("parallel"),
        name="peer_select",
    )(h2, g_ffn, w_pq, k1, k2)


GATE_ROWS = 32
MXU_COLS = 256


def _peer_gate_block(at_ref, wt_ref, z_ref, group, sub0, c, sg, *, te):
    n_i1 = te // PEER_NKEYS
    groups_per_block = GATE_ROWS // 8
    gs = slice(sg * groups_per_block, (sg + 1) * groups_per_block)
    gate = [None] * n_i1
    for hd in range(PEER_HEADS):
        rank2 = z_ref[c, hd, gs, Z_RANK2]
        e2 = z_ref[c, hd, gs, Z_E2]
        for r in range(n_i1):
            rs = slice(sub0 + r, sub0 + r + 1)
            e1 = z_ref[c, hd, group, Z_E1, rs, :]
            cnt = z_ref[c, hd, group, Z_CNT, rs, :]
            term = e1 * jnp.where(rank2 < cnt, e2, 0.0)
            gate[r] = term if gate[r] is None else gate[r] + term
    for r in range(n_i1):
        es = slice(r * PEER_NKEYS + sg * GATE_ROWS, r * PEER_NKEYS + (sg + 1) * GATE_ROWS)
        a = at_ref[c, es, :]
        act = a * (1.0 + lax.erf(a * math.sqrt(0.5)))
        wt_ref[c, es, :] = (gate[r].reshape(GATE_ROWS, LANES) * act).astype(BF16)


def _peer_dense_kernel(xn_ref, u_ref, vt_ref, z_ref, h_ref, gf_ref,
                       o_ref, acc_ref, at0_ref, at1_ref, wt0_ref, wt1_ref, *, te, tm, n_tiles):
    j = pl.program_id(1)

    @pl.when(j == 0)
    def _():
        acc_ref[...] = jnp.zeros_like(acc_ref)

    rows_per_step = te // PEER_NKEYS
    assert 2 * rows_per_step == 8 and n_tiles % 2 == 0

    n_chunks = tm // LANES
    assert n_chunks == 4 and 2 * LANES == MXU_COLS
    n_sg = PEER_NKEYS // GATE_ROWS
    d_piece = acc_ref.shape[1] // (n_chunks * n_sg // 2)

    def step(parity, do_a=True, do_b=True, do_c=True):
        at_w, at_r = (at0_ref, at1_ref) if parity == 0 else (at1_ref, at0_ref)
        wt_w, wt_r = (wt1_ref, wt0_ref) if parity == 0 else (wt0_ref, wt1_ref)
        group, sub0 = (j - 1) // 2, (rows_per_step if parity == 0 else 0)
        for c in range(n_chunks):
            for sg in range(n_sg):
                k = c * n_sg + sg
                p, dr = k % 2, slice((k // 2) * d_piece, (k // 2 + 1) * d_piece)
                if do_c:
                    w = jnp.concatenate([wt_r[2 * p], wt_r[2 * p + 1]], axis=1)
                    res = _dot(vt_ref[dr, :], w)
                    acc_ref[2 * p, dr, :] += res[:, :LANES]
                    acc_ref[2 * p + 1, dr, :] += res[:, LANES:]
                if do_b:
                    _peer_gate_block(at_r, wt_w, z_ref, group, sub0, c, sg, te=te)
            if do_a:
                c0, er = 2 * (c // 2), slice((c % 2) * (te // 2), (c % 2 + 1) * (te // 2))
                res = _dot_nt(u_ref[er, :], xn_ref[c0 * LANES:(c0 + 2) * LANES, :])
                at_w[c0, er, :] = res[:, :LANES]
                at_w[c0 + 1, er, :] = res[:, LANES:]

    last = n_tiles + 1
    pl.when(j == 0)(lambda: step(0, do_b=False, do_c=False))
    pl.when(j == 1)(lambda: step(1, do_c=False))
    pl.when((j % 2 == 0) & (j >= 2) & (j < n_tiles))(lambda: step(0))
    pl.when((j % 2 == 1) & (j >= 2) & (j < n_tiles))(lambda: step(1))
    pl.when(j == n_tiles)(lambda: step(0, do_a=False))
    pl.when(j == last)(lambda: step(1, do_a=False, do_b=False))

    @pl.when(j == last)
    def _():
        for c in range(n_chunks):
            rows = slice(c * LANES, (c + 1) * LANES)
            o_ref[rows, :] = _rms(h_ref[rows, :] + acc_ref[c].T, gf_ref[...])


PEER_TE = 512


def _peer_dense(xn, u, vt, z, h2, g_final):
    T, D = h2.shape
    tm, te = 512, PEER_TE
    n_tiles = vt.shape[0]
    n_chunks = tm // LANES
    clamp = lambda t: jnp.minimum(jnp.maximum(t, 0), n_tiles - 1)
    row = pl.BlockSpec((tm, D), lambda i, j: (i, 0))
    return pl.pallas_call(
        functools.partial(_peer_dense_kernel, te=te, tm=tm, n_tiles=n_tiles),
        grid=(T // tm, n_tiles + 2),
        in_specs=[row, pl.BlockSpec((te, D), lambda i, j: (clamp(j), 0)),
                  pl.BlockSpec((None, D, te), lambda i, j: (clamp(j - 2), 0, 0)),
                  pl.BlockSpec((n_chunks,) + z.shape[1:], lambda i, j: (i, 0, 0, 0, 0, 0)),
                  row, pl.BlockSpec((1, D), lambda i, j: (0, 0))],
        out_specs=row,
        out_shape=jax.ShapeDtypeStruct((T, D), F32),
        scratch_shapes=[pltpu.VMEM((n_chunks, D, LANES), F32),
                        pltpu.VMEM((n_chunks, te, LANES), F32), pltpu.VMEM((n_chunks, te, LANES), F32),
                        pltpu.VMEM((n_chunks, te, LANES), BF16), pltpu.VMEM((n_chunks, te, LANES), BF16)],
        compiler_params=_params("parallel", "arbitrary"),
        name="peer_dense",
    )(xn, u, vt, z, h2, g_final)


def _layer(h, mem2, tabs, g_attn, w_in, g_q_a, w_uq, g_kv_a, w_ukv, g_mla_o, g_moba_o, w_out, g_cross,
           g_mem, w_cq, w_ck, w_cv, w_co, g_ffn, w_pq, sub_keys1, sub_keys2, expert_u, expert_v,
           g_final, *, batch, seq, mem_len):
    D = h.shape[1]
    row = lambda g: g.reshape(1, -1)
    n_lat = MLA_Q_RANK + MLA_KV_RANK + MLA_ROPE
    w_in_p = jnp.concatenate([w_in[:, :n_lat], jnp.zeros((D, LANES - MLA_ROPE), w_in.dtype),
                              w_in[:, n_lat:]], axis=1).astype(BF16)
    w_uq_p = jnp.pad(w_uq.reshape(MLA_Q_RANK, MLA_HEADS, MLA_NOPE + MLA_ROPE),
                     ((0, 0), (0, 0), (0, MLA_QK_PAD - MLA_NOPE - MLA_ROPE))
                     ).reshape(MLA_Q_RANK, MLA_HEADS * MLA_QK_PAD).astype(BF16)
    w_ukv3 = w_ukv.reshape(MLA_KV_RANK, MLA_HEADS, MLA_NOPE + MLA_V)
    w_uk = w_ukv3[:, :, :MLA_NOPE].reshape(MLA_KV_RANK, MLA_HEADS * MLA_NOPE).astype(BF16)
    w_uv = w_ukv3[:, :, MLA_NOPE:].reshape(MLA_KV_RANK, MLA_HEADS * MLA_V).astype(BF16)

    ql, kl, vl, qm, km, vm, sel = _attn_prep(h, row(g_attn), w_in_p, row(g_q_a), w_uq_p, row(g_kv_a),
                                             w_uk, w_uv, tabs, seq=seq)
    o_mla = _mla_attention(ql, kl, vl, batch=batch, seq=seq)
    o_moba = _moba_attention(qm, km, vm, sel, batch=batch, seq=seq)
    h1 = _out_proj(h, o_mla, o_moba, row(g_mla_o), row(g_moba_o), w_out.astype(BF16))

    w_ckv = jnp.concatenate([w_ck, w_cv], axis=1).astype(BF16)
    kv = _rms_matmul(mem2, row(g_mem), w_ckv, BF16, tm=min(mem2.shape[0], 512), tn=1024)
    h2 = _cross_attention(h1, row(g_cross), w_cq.astype(BF16), kv, w_co.astype(BF16), seq=seq, mem_len=mem_len)

    xn, z = _peer_select(h2, row(g_ffn), w_pq.astype(BF16), sub_keys1, sub_keys2)
    vt = expert_v.reshape(-1, PEER_TE, D).transpose(0, 2, 1).astype(BF16)
    return _peer_dense(xn, expert_u.astype(BF16), vt, z, h2, row(g_final))


def kernel(x, mem, positions, g_attn, w_in, g_q_a, w_uq, g_kv_a, w_ukv, g_mla_o, g_moba_o, w_out, g_cross,
           g_mem, w_cq, w_ck, w_cv, w_co, g_ffn, w_pq, sub_keys1, sub_keys2, expert_u, expert_v, g_final):
    B, S, D = x.shape
    M = mem.shape[1]
    depth = w_in.shape[0]
    assert depth == 1, "the final norm is fused into the layer's last kernel"
    tabs = _rope_tables(positions.reshape(B * S, 1))
    out = _layer(x.reshape(B * S, D), mem.reshape(B * M, D), tabs, g_attn[0], w_in[0], g_q_a[0], w_uq[0],
                 g_kv_a[0], w_ukv[0], g_mla_o[0], g_moba_o[0], w_out[0], g_cross[0], g_mem[0], w_cq[0],
                 w_ck[0], w_cv[0], w_co[0], g_ffn[0], w_pq[0], sub_keys1[0], sub_keys2[0], expert_u[0],
                 expert_v[0], g_final, batch=B, seq=S, mem_len=M)
    return out.reshape(B, S, D)
```

```python
import functools
import math

import jax
import jax.numpy as jnp
from jax import lax
from jax.experimental import pallas as pl
from jax.experimental.pallas import tpu as pltpu

F32 = jnp.float32
BF16 = jnp.bfloat16

EPS = 1e-6
ROPE_THETA = 10000.0
MLA_HEADS = 8
MLA_Q_RANK = 512
MLA_KV_RANK = 256
MLA_NOPE = 128
MLA_ROPE = 64
MLA_V = 128
MLA_QK_PAD = 256
MOBA_HEADS = 8
MOBA_HEAD_DIM = 128
MOBA_BLOCK = 256
MOBA_TOPK = 3
CROSS_HEADS = 4
PEER_HEADS = 8
PEER_NKEYS = 128
PEER_DKEY = 256
PEER_TOPK = 16

LANES = 128
VMEM_LIMIT = 56 * 1024 * 1024

NEG_INF = float("-inf")
NT_DIMS = (((1,), (1,)), ((), ()))


def _rms(x, g):
    ms = jnp.mean(x * x, axis=-1, keepdims=True)
    return x * lax.rsqrt(ms + EPS) * g


def _dot(a, b):
    return jnp.dot(a, b, preferred_element_type=F32)


def _dot_nt(a, b, precision=None):
    return lax.dot_general(a, b, NT_DIMS, precision=precision, preferred_element_type=F32)


def _const_spec(shape):
    nd = len(shape)
    return pl.BlockSpec(shape, lambda *_: (0,) * nd, pipeline_mode=pl.Buffered(1))


def _params(*sem):
    return pltpu.CompilerParams(dimension_semantics=sem, vmem_limit_bytes=VMEM_LIMIT)


def _rope_table_kernel(pos_ref, cm_ref, sm_ref, cl_ref, sl_ref):
    pos = pos_ref[...].astype(F32)
    lane = lax.broadcasted_iota(jnp.int32, (1, LANES), 1)
    half = MOBA_HEAD_DIM // 2
    inv = jnp.exp((lane % half).astype(F32) * (-2.0 * math.log(ROPE_THETA) / MOBA_HEAD_DIM))
    ang = pos * inv
    cm_ref[...] = jnp.cos(ang)
    sn = jnp.sin(ang)
    sm_ref[...] = jnp.where(lane < half, -sn, sn)
    half = MLA_ROPE // 2
    inv = jnp.exp((lane % half).astype(F32) * (-2.0 * math.log(ROPE_THETA) / MLA_ROPE))
    ang = pos * inv
    valid = lane < MLA_ROPE
    cl_ref[...] = jnp.where(valid, jnp.cos(ang), 0.0)
    sn = jnp.sin(ang)
    sl_ref[...] = jnp.where(lane < half, -sn, jnp.where(valid, sn, 0.0))


def _rope_tables(pos_col):
    T = pos_col.shape[0]
    tm = 512
    out = jax.ShapeDtypeStruct((T, LANES), F32)
    spec = pl.BlockSpec((tm, LANES), lambda i: (i, 0))
    return pl.pallas_call(
        _rope_table_kernel,
        grid=(T // tm,),
        in_specs=[pl.BlockSpec((tm, 1), lambda i: (i, 0))],
        out_specs=[spec] * 4,
        out_shape=[out] * 4,
        compiler_params=_params("parallel"),
        name="rope_tables",
    )(pos_col)


def _rope_moba(x, cos, sin):
    return x * cos + pltpu.roll(x, MOBA_HEAD_DIM // 2, 1) * sin


def _rope_mla(x, cos, sin, lane):
    half = MLA_ROPE // 2
    partner = jnp.where(lane < half, pltpu.roll(x, LANES - half, 1), pltpu.roll(x, half, 1))
    return x * cos + partner * sin


C_COLS = MLA_Q_RANK + MLA_KV_RANK + LANES
MOBA_W = MOBA_HEADS * MOBA_HEAD_DIM


def _attn_prep_kernel(x_ref, g_ref, w_in_ref, gq_ref, wuq_ref, gkv_ref, wuk_ref, wuv_ref,
                      cm_ref, sm_ref, cl_ref, sl_ref,
                      ql_ref, kl_ref, vl_ref, qm_ref, km_ref, vm_ref, sel_ref,
                      kbar_ref, qmf_ref, kmf_ref, *, tm, tiles_per_seq, n_blocks):
    i = pl.program_id(0)

    @pl.when(i == 0)
    def _():
        kbar_ref[...] = jnp.zeros_like(kbar_ref)

    xn = _rms(x_ref[...], g_ref[...]).astype(BF16)
    lane = lax.broadcasted_iota(jnp.int32, (1, LANES), 1)
    cl, sl = cl_ref[...], sl_ref[...]
    cm, sm = cm_ref[...], sm_ref[...]

    c = _dot(xn, w_in_ref[:, 0:C_COLS])
    cq = _rms(c[:, 0:MLA_Q_RANK], gq_ref[...]).astype(BF16)
    q = _dot(cq, wuq_ref[...])
    ckv = _rms(c[:, MLA_Q_RANK:MLA_Q_RANK + MLA_KV_RANK], gkv_ref[...]).astype(BF16)
    kn = _dot(ckv, wuk_ref[...])
    vl_ref[...] = _dot(ckv, wuv_ref[...]).astype(BF16)
    kr = _rope_mla(c[:, MLA_Q_RANK + MLA_KV_RANK:C_COLS], cl, sl, lane).astype(BF16)
    for h in range(MLA_HEADS):
        o = h * MLA_QK_PAD
        ql_ref[:, o:o + LANES] = q[:, o:o + LANES].astype(BF16)
        ql_ref[:, o + LANES:o + 2 * LANES] = _rope_mla(q[:, o + LANES:o + 2 * LANES], cl, sl, lane).astype(BF16)
        kl_ref[:, o:o + LANES] = kn[:, h * LANES:(h + 1) * LANES].astype(BF16)
        kl_ref[:, o + LANES:o + 2 * LANES] = kr

    o0 = C_COLS
    qm = _dot(xn, w_in_ref[:, o0:o0 + MOBA_W])
    km = _dot(xn, w_in_ref[:, o0 + MOBA_W:o0 + 2 * MOBA_W])
    vm_ref[...] = _dot(xn, w_in_ref[:, o0 + 2 * MOBA_W:o0 + 3 * MOBA_W]).astype(BF16)
    for h in range(MOBA_HEADS):
        sl_h = slice(h * LANES, (h + 1) * LANES)
        qr = _rope_moba(qm[:, sl_h], cm, sm)
        kr_h = _rope_moba(km[:, sl_h], cm, sm)
        qmf_ref[:, sl_h] = qr
        kmf_ref[:, sl_h] = kr_h
        qm_ref[:, sl_h] = qr.astype(BF16)
        km_ref[:, sl_h] = kr_h.astype(BF16)

    blk0 = (i % tiles_per_seq) * (tm // MOBA_BLOCK)
    for s in range(tm // MOBA_BLOCK):
        kbar_ref[pl.ds(blk0 + s, 1), :] = jnp.mean(
            kmf_ref[s * MOBA_BLOCK:(s + 1) * MOBA_BLOCK, :], axis=0, keepdims=True)
    tok = lax.broadcasted_iota(jnp.int32, (n_blocks, tm), 1)
    own = blk0 + tok // MOBA_BLOCK
    blk = lax.broadcasted_iota(jnp.int32, (n_blocks, tm), 0)
    past = blk < own
    rows = []
    for h in range(MOBA_HEADS):
        sl_h = slice(h * LANES, (h + 1) * LANES)
        gate = _dot_nt(kbar_ref[:, sl_h], qmf_ref[:, sl_h], precision=lax.Precision.HIGHEST)
        gate = jnp.where(past, gate, NEG_INF)
        ahead = jnp.zeros((n_blocks, tm), jnp.int32)
        for m in range(n_blocks):
            gm = gate[m:m + 1, :]
            beats = (gm > gate) | ((gm == gate) & (m < blk))
            ahead = ahead + beats.astype(jnp.int32)
        rows.append(jnp.where(past & (ahead < MOBA_TOPK), 1.0, 0.0).astype(F32))
    pad = jnp.zeros((LANES - MOBA_HEADS * n_blocks, tm), F32)
    sel_ref[...] = jnp.concatenate(rows + [pad], axis=0).T


def _attn_prep(x2, g_attn, w_in_p, g_q, w_uq_p, g_kv, w_uk, w_uv, tabs, *, seq):
    T, D = x2.shape
    tm = 512
    n_blocks = seq // MOBA_BLOCK
    assert seq % tm == 0 and tm % MOBA_BLOCK == 0 and MOBA_HEADS * n_blocks <= LANES
    cm, sm, cl, sl = tabs
    row = lambda w: pl.BlockSpec((tm, w), lambda i: (i, 0))
    kern = functools.partial(_attn_prep_kernel, tm=tm, tiles_per_seq=seq // tm, n_blocks=n_blocks)
    outs = [(MLA_HEADS * MLA_QK_PAD, BF16), (MLA_HEADS * MLA_QK_PAD, BF16), (MLA_HEADS * MLA_V, BF16),
            (MOBA_W, BF16), (MOBA_W, BF16), (MOBA_W, BF16), (LANES, F32)]
    return pl.pallas_call(
        kern,
        grid=(T // tm,),
        in_specs=[row(D), _const_spec(g_attn.shape), _const_spec(w_in_p.shape), _const_spec(g_q.shape),
                  _const_spec(w_uq_p.shape), _const_spec(g_kv.shape), _const_spec(w_uk.shape),
                  _const_spec(w_uv.shape), row(LANES), row(LANES), row(LANES), row(LANES)],
        out_specs=[row(w) for w, _ in outs],
        out_shape=[jax.ShapeDtypeStruct((T, w), dt) for w, dt in outs],
        scratch_shapes=[pltpu.VMEM((n_blocks, MOBA_W), F32), pltpu.VMEM((tm, MOBA_W), F32),
                        pltpu.VMEM((tm, MOBA_W), F32)],
        compiler_params=_params("arbitrary"),
        name="attn_prep",
    )(x2, g_attn, w_in_p, g_q, w_uq_p, g_kv, w_uk, w_uv, cm, sm, cl, sl)


ATTN_TQ = 256


def _softmax_av(s, v):
    m = jnp.max(s, axis=-1, keepdims=True)
    e = jnp.exp(s - m)
    l = jnp.sum(e, axis=-1, keepdims=True)
    return _dot(e.astype(BF16), v) / l


def _causal_bias(tq):
    r = lax.broadcasted_iota(jnp.int32, (tq, tq), 0)
    c = lax.broadcasted_iota(jnp.int32, (tq, tq), 1)
    return jnp.where(c <= r, 0.0, NEG_INF).astype(F32)


def _pipelined_tiles(scores, attend, o_ref, n_tiles, tq):
    s_next = scores(0)
    for i in range(n_tiles):
        s = s_next
        if i + 1 < n_tiles:
            s_next = scores(i + 1)
        o_ref[i * tq:(i + 1) * tq, :] = attend(s, (i + 1) * tq)


def _mla_attn_kernel(q_ref, k_ref, v_ref, o_ref, *, seq, scale):
    tq = ATTN_TQ
    diag = _causal_bias(tq)

    def scores(i):
        n = (i + 1) * tq
        s = _dot_nt(q_ref[i * tq:n, :], k_ref[0:n, :]) * scale
        return s + jnp.concatenate([jnp.zeros((tq, i * tq), F32), diag], axis=1) if i else s + diag

    _pipelined_tiles(scores, lambda s, n: _softmax_av(s, v_ref[0:n, :]), o_ref, seq // tq, tq)


def _mla_attention(ql, kl, vl, *, batch, seq):
    T = ql.shape[0]
    scale = (MLA_NOPE + MLA_ROPE) ** -0.5
    return pl.pallas_call(
        functools.partial(_mla_attn_kernel, seq=seq, scale=scale),
        grid=(batch, MLA_HEADS),
        in_specs=[pl.BlockSpec((seq, MLA_QK_PAD), lambda b, h: (b, h)),
                  pl.BlockSpec((seq, MLA_QK_PAD), lambda b, h: (b, h)),
                  pl.BlockSpec((seq, MLA_V), lambda b, h: (b, h))],
        out_specs=pl.BlockSpec((seq, MLA_V), lambda b, h: (b, h)),
        out_shape=jax.ShapeDtypeStruct((T, MLA_HEADS * MLA_V), F32),
        compiler_params=_params("parallel", "parallel"),
        name="mla_attention",
    )(ql, kl, vl)


def _moba_attn_kernel(q_ref, k_ref, v_ref, sel_ref, o_ref, *, seq, scale, n_blocks):
    tq = ATTN_TQ
    h = pl.program_id(1)
    lane = lax.broadcasted_iota(jnp.int32, (1, LANES), 1)
    diag = _causal_bias(tq)

    def scores(i):
        n = (i + 1) * tq
        s = _dot_nt(q_ref[i * tq:n, :], k_ref[0:n, :]) * scale
        sel = sel_ref[i * tq:n, :]
        pieces = []
        for b in range(i):
            chosen = jnp.sum(jnp.where(lane == h * n_blocks + b, sel, 0.0), axis=-1, keepdims=True)
            pieces.append(jnp.broadcast_to(jnp.where(chosen > 0.0, 0.0, NEG_INF), (tq, tq)))
        return s + (jnp.concatenate(pieces + [diag], axis=1) if pieces else diag)

    _pipelined_tiles(scores, lambda s, n: _softmax_av(s, v_ref[0:n, :]), o_ref, seq // tq, tq)


def _moba_attention(qm, km, vm, sel, *, batch, seq):
    T = qm.shape[0]
    assert ATTN_TQ == MOBA_BLOCK
    hd = pl.BlockSpec((seq, MOBA_HEAD_DIM), lambda b, h: (b, h))
    return pl.pallas_call(
        functools.partial(_moba_attn_kernel, seq=seq, scale=MOBA_HEAD_DIM ** -0.5,
                          n_blocks=seq // MOBA_BLOCK),
        grid=(batch, MOBA_HEADS),
        in_specs=[hd, hd, hd, pl.BlockSpec((seq, LANES), lambda b, h: (b, 0))],
        out_specs=hd,
        out_shape=jax.ShapeDtypeStruct((T, MOBA_W), F32),
        compiler_params=_params("parallel", "parallel"),
        name="moba_attention",
    )(qm, km, vm, sel)


def _out_proj_kernel(x_ref, ol_ref, om_ref, gl_ref, gm_ref, w_ref, h_ref, *, wl):
    a = _rms(ol_ref[...], gl_ref[...]).astype(BF16)
    b = _rms(om_ref[...], gm_ref[...]).astype(BF16)
    h_ref[...] = x_ref[...] + (_dot(a, w_ref[0:wl, :]) + _dot(b, w_ref[wl:, :]))


def _out_proj(x2, o_mla, o_moba, g_l, g_m, w_out):
    T, D = x2.shape
    tm = 512
    row = lambda w: pl.BlockSpec((tm, w), lambda i: (i, 0))
    return pl.pallas_call(
        functools.partial(_out_proj_kernel, wl=o_mla.shape[1]),
        grid=(T // tm,),
        in_specs=[row(D), row(o_mla.shape[1]), row(o_moba.shape[1]), _const_spec(g_l.shape),
                  _const_spec(g_m.shape), _const_spec(w_out.shape)],
        out_specs=row(D),
        out_shape=jax.ShapeDtypeStruct((T, D), F32),
        compiler_params=_params("parallel"),
        name="out_proj",
    )(x2, o_mla, o_moba, g_l, g_m, w_out)


def _rms_matmul_kernel(x_ref, g_ref, w_ref, o_ref):
    xn = _rms(x_ref[...], g_ref[...]).astype(BF16)
    o_ref[...] = _dot(xn, w_ref[...]).astype(o_ref.dtype)


def _rms_matmul(x, g, w, out_dtype, *, tm, tn):
    M, K = x.shape
    N = w.shape[1]
    return pl.pallas_call(
        _rms_matmul_kernel,
        grid=(M // tm, N // tn),
        in_specs=[pl.BlockSpec((tm, K), lambda i, j: (i, 0)), pl.BlockSpec((1, K), lambda i, j: (0, 0)),
                  pl.BlockSpec((K, tn), lambda i, j: (0, j))],
        out_specs=pl.BlockSpec((tm, tn), lambda i, j: (i, j)),
        out_shape=jax.ShapeDtypeStruct((M, N), out_dtype),
        compiler_params=_params("parallel", "parallel"),
        name="rms_matmul",
    )(x, g, w)


def _cross_kernel(h_ref, g_ref, wq_ref, kv_ref, wo_ref, o_ref, *, scale):
    h = h_ref[...]
    D = h.shape[1]
    dh = D // CROSS_HEADS
    q = _dot(_rms(h, g_ref[...]).astype(BF16), wq_ref[...]).astype(BF16)
    outs = []
    for hd in range(CROSS_HEADS):
        s = _dot_nt(q[:, hd * dh:(hd + 1) * dh], kv_ref[:, hd * dh:(hd + 1) * dh]) * scale
        outs.append(_softmax_av(s, kv_ref[:, D + hd * dh:D + (hd + 1) * dh]).astype(BF16))
    o_ref[...] = h + _dot(jnp.concatenate(outs, axis=1), wo_ref[...])


def _cross_attention(h1, g_cross, w_cq, kv, w_co, *, seq, mem_len):
    T, D = h1.shape
    tm = 256
    tiles_per_seq = seq // tm
    row = pl.BlockSpec((tm, D), lambda i: (i, 0))
    return pl.pallas_call(
        functools.partial(_cross_kernel, scale=(D // CROSS_HEADS) ** -0.5),
        grid=(T // tm,),
        in_specs=[row, _const_spec(g_cross.shape), _const_spec(w_cq.shape),
                  pl.BlockSpec((mem_len, 2 * D), lambda i: (i // tiles_per_seq, 0)),
                  _const_spec(w_co.shape)],
        out_specs=row,
        out_shape=jax.ShapeDtypeStruct((T, D), F32),
        compiler_params=_params("parallel"),
        name="cross_attention",
    )(h1, g_cross, w_cq, kv, w_co)


def _top16_cols(s, exact):
    n, tm = s.shape
    idx = lax.broadcasted_iota(jnp.int32, (n, tm), 0).astype(F32) if exact else None
    rank = jnp.full((n, tm), float(PEER_TOPK), F32)
    vals = []
    for r in range(PEER_TOPK):
        m = jnp.max(s, axis=0, keepdims=True)
        hit = s == m
        if exact:
            hit = idx == jnp.min(jnp.where(hit, idx, float(n)), axis=0, keepdims=True)
        rank = jnp.where(hit, float(r), rank)
        s = jnp.where(hit, NEG_INF, s)
        vals.append(m)
    kept = jnp.sum(jnp.where(rank < float(PEER_TOPK), 1.0, 0.0), axis=0, keepdims=True)
    return vals, rank, kept == float(PEER_TOPK)


_PAIR_B = [PEER_TOPK // (a + 1) for a in range(8)]


def _pair_counts(v1, v2):
    tm = v1[0].shape[1]
    v1s = jnp.concatenate(v1, axis=0)
    v2s = jnp.concatenate(v2, axis=0)
    pieces, keys = [], []
    for a, nb in enumerate(_PAIR_B):
        rows = -(-nb // 8) * 8
        r = lax.broadcasted_iota(jnp.int32, (rows, tm), 0)
        pieces.append(jnp.where(r < nb, v2s[0:rows, :] + v1[a], NEG_INF))
        keys.append((a * PEER_TOPK + r).astype(F32))
    r = lax.broadcasted_iota(jnp.int32, (8, tm), 0)
    pieces.append(v1s[8:16, :] + v2[0])
    keys.append(((r + 8) * PEER_TOPK).astype(F32))
    cand = jnp.concatenate(pieces, axis=0)
    key = jnp.concatenate(keys, axis=0)
    work = cand
    for _ in range(PEER_TOPK):
        m = jnp.max(work, axis=0, keepdims=True)
        first = jnp.min(jnp.where(work == m, key, float(PEER_TOPK * PEER_TOPK)), axis=0, keepdims=True)
        work = jnp.where(key == first, NEG_INF, work)
    chosen = work != cand
    top = v1[0] + v2[0]
    z = jnp.sum(jnp.where(chosen, jnp.exp(cand - top), 0.0), axis=0, keepdims=True)
    picked = jnp.where(chosen, 1.0, 0.0)
    counts, o = [], 0
    for nb in _PAIR_B:
        rows = -(-nb // 8) * 8
        counts.append(jnp.sum(picked[o:o + rows, :], axis=0, keepdims=True))
        o += rows
    for a in range(8):
        counts.append(picked[o + a:o + a + 1, :])
    return counts, z


Z_E1, Z_CNT, Z_RANK2, Z_E2 = range(4)
Z_GROUPS = PEER_NKEYS // 8


def _peer_retrieve_unit(q_ref, k1_ref, k2_ref, z_ref, c, hd, exact):
    half = PEER_DKEY // 2
    hi = lax.Precision.HIGHEST
    ts = slice(c * LANES, (c + 1) * LANES)
    o = hd * PEER_DKEY
    tiled = lambda val: val.reshape(Z_GROUPS, 8, LANES)
    s1 = _dot_nt(k1_ref[...], q_ref[ts, o:o + half], precision=hi)
    v1, rank1, ok1 = _top16_cols(s1, exact)
    z_ref[c, hd, :, Z_E1] = tiled(jnp.exp(s1 - v1[0]))
    z_ref[c, hd, :, Z_CNT] = tiled(rank1)
    s2 = _dot_nt(k2_ref[...], q_ref[ts, o + half:o + 2 * half], precision=hi)
    v2, rank2, ok2 = _top16_cols(s2, exact)
    z_ref[c, hd, :, Z_RANK2] = tiled(rank2)
    z_ref[c, hd, :, Z_E2] = tiled(jnp.exp(s2 - v2[0]))
    counts, z = _pair_counts(v1, v2)
    rank1 = z_ref[c, hd, :, Z_CNT]
    cnt = jnp.zeros(rank1.shape, F32)
    for a in range(PEER_TOPK):
        cnt = jnp.where(rank1 == float(a), counts[a], cnt)
    z_ref[c, hd, :, Z_CNT] = cnt
    z_ref[c, hd, :, Z_E2] = z_ref[c, hd, :, Z_E2] / z
    return jnp.where(ok1 & ok2, 0.0, 1.0)


def _peer_select_kernel(h_ref, g_ref, wq_ref, k1_ref, k2_ref, xn_ref, z_ref, q_ref):
    xn = _rms(h_ref[...], g_ref[...]).astype(BF16)
    xn_ref[...] = xn
    q_ref[...] = _dot(xn, wq_ref[...])
    units = [(c, hd) for c in range(q_ref.shape[0] // LANES) for hd in range(PEER_HEADS)]
    redo = [jnp.max(_peer_retrieve_unit(q_ref, k1_ref, k2_ref, z_ref, c, hd, exact=False)) > 0.0
            for c, hd in units]
    for (c, hd), flag in zip(units, redo):
        @pl.when(flag)
        def _(c=c, hd=hd):
            _peer_retrieve_unit(q_ref, k1_ref, k2_ref, z_ref, c, hd, exact=True)


def _peer_select(h2, g_ffn, w_pq, k1, k2):
    T, D = h2.shape
    tm = 256
    row = pl.BlockSpec((tm, D), lambda i: (i, 0))
    z_block = (tm // LANES, PEER_HEADS, Z_GROUPS, 4, 8, LANES)
    return pl.pallas_call(
        _peer_select_kernel,
        grid=(T // tm,),
        in_specs=[row, _const_spec(g_ffn.shape), _const_spec(w_pq.shape), _const_spec(k1.shape),
                  _const_spec(k2.shape)],
        out_specs=[row, pl.BlockSpec(z_block, lambda i: (i, 0, 0, 0, 0, 0))],
        out_shape=[jax.ShapeDtypeStruct((T, D), BF16),
                   jax.ShapeDtypeStruct((T // LANES,) + z_block[1:], F32)],
        scratch_shapes=[pltpu.VMEM((tm, w_pq.shape[1]), F32)],
        compiler_params=_params("parallel"),
        name="peer_select",
    )(h2, g_ffn, w_pq, k1, k2)


GATE_ROWS = 32
MXU_COLS = 256


def _peer_gate_block(at_ref, wt_ref, z_ref, group, sub0, c, sg, *, te):
    n_i1 = te // PEER_NKEYS
    groups_per_block = GATE_ROWS // 8
    gs = slice(sg * groups_per_block, (sg + 1) * groups_per_block)
    gate = [None] * n_i1
    for hd in range(PEER_HEADS):
        rank2 = z_ref[c, hd, gs, Z_RANK2]
        e2 = z_ref[c, hd, gs, Z_E2]
        for r in range(n_i1):
            rs = slice(sub0 + r, sub0 + r + 1)
            e1 = z_ref[c, hd, group, Z_E1, rs, :]
            cnt = z_ref[c, hd, group, Z_CNT, rs, :]
            term = e1 * jnp.where(rank2 < cnt, e2, 0.0)
            gate[r] = term if gate[r] is None else gate[r] + term
    for r in range(n_i1):
        es = slice(r * PEER_NKEYS + sg * GATE_ROWS, r * PEER_NKEYS + (sg + 1) * GATE_ROWS)
        a = at_ref[c, es, :]
        act = 0.5 * a * (1.0 + lax.erf(a * math.sqrt(0.5)))
        wt_ref[c, es, :] = (gate[r].reshape(GATE_ROWS, LANES) * act).astype(BF16)


def _peer_dense_kernel(xn_ref, u_ref, vt_ref, z_ref, h_ref, gf_ref,
                       o_ref, acc_ref, at0_ref, at1_ref, wt0_ref, wt1_ref, *, te, tm, n_tiles):
    j = pl.program_id(1)

    @pl.when(j == 0)
    def _():
        acc_ref[...] = jnp.zeros_like(acc_ref)

    rows_per_step = te // PEER_NKEYS
    assert 2 * rows_per_step == 8 and n_tiles % 2 == 0

    n_chunks = tm // LANES
    assert n_chunks == 4 and 2 * LANES == MXU_COLS
    n_sg = PEER_NKEYS // GATE_ROWS
    d_piece = acc_ref.shape[1] // (n_chunks * n_sg // 2)

    def step(parity, do_a=True, do_b=True, do_c=True):
        at_w, at_r = (at0_ref, at1_ref) if parity == 0 else (at1_ref, at0_ref)
        wt_w, wt_r = (wt1_ref, wt0_ref) if parity == 0 else (wt0_ref, wt1_ref)
        group, sub0 = (j - 1) // 2, (rows_per_step if parity == 0 else 0)
        for c in range(n_chunks):
            for sg in range(n_sg):
                k = c * n_sg + sg
                p, dr = k % 2, slice((k // 2) * d_piece, (k // 2 + 1) * d_piece)
                if do_c:
                    w = jnp.concatenate([wt_r[2 * p], wt_r[2 * p + 1]], axis=1)
                    res = _dot(vt_ref[dr, :], w)
                    acc_ref[2 * p, dr, :] += res[:, :LANES]
                    acc_ref[2 * p + 1, dr, :] += res[:, LANES:]
                if do_b:
                    _peer_gate_block(at_r, wt_w, z_ref, group, sub0, c, sg, te=te)
            if do_a:
                c0, er = 2 * (c // 2), slice((c % 2) * (te // 2), (c % 2 + 1) * (te // 2))
                res = _dot_nt(u_ref[er, :], xn_ref[c0 * LANES:(c0 + 2) * LANES, :])
                at_w[c0, er, :] = res[:, :LANES]
                at_w[c0 + 1, er, :] = res[:, LANES:]

    last = n_tiles + 1
    pl.when(j == 0)(lambda: step(0, do_b=False, do_c=False))
    pl.when(j == 1)(lambda: step(1, do_c=False))
    pl.when((j % 2 == 0) & (j >= 2) & (j < n_tiles))(lambda: step(0))
    pl.when((j % 2 == 1) & (j >= 2) & (j < n_tiles))(lambda: step(1))
    pl.when(j == n_tiles)(lambda: step(0, do_a=False))
    pl.when(j == last)(lambda: step(1, do_a=False, do_b=False))

    @pl.when(j == last)
    def _():
        for c in range(n_chunks):
            rows = slice(c * LANES, (c + 1) * LANES)
            o_ref[rows, :] = _rms(h_ref[rows, :] + acc_ref[c].T, gf_ref[...])


PEER_TE = 512


def _peer_dense(xn, u, vt, z, h2, g_final):
    T, D = h2.shape
    tm, te = 512, PEER_TE
    n_tiles = vt.shape[0]
    n_chunks = tm // LANES
    clamp = lambda t: jnp.minimum(jnp.maximum(t, 0), n_tiles - 1)
    row = pl.BlockSpec((tm, D), lambda i, j: (i, 0))
    return pl.pallas_call(
        functools.partial(_peer_dense_kernel, te=te, tm=tm, n_tiles=n_tiles),
        grid=(T // tm, n_tiles + 2),
        in_specs=[row, pl.BlockSpec((te, D), lambda i, j: (clamp(j), 0)),
                  pl.BlockSpec((None, D, te), lambda i, j: (clamp(j - 2), 0, 0)),
                  pl.BlockSpec((n_chunks,) + z.shape[1:], lambda i, j: (i, 0, 0, 0, 0, 0)),
                  row, pl.BlockSpec((1, D), lambda i, j: (0, 0))],
        out_specs=row,
        out_shape=jax.ShapeDtypeStruct((T, D), F32),
        scratch_shapes=[pltpu.VMEM((n_chunks, D, LANES), F32),
                        pltpu.VMEM((n_chunks, te, LANES), F32), pltpu.VMEM((n_chunks, te, LANES), F32),
                        pltpu.VMEM((n_chunks, te, LANES), BF16), pltpu.VMEM((n_chunks, te, LANES), BF16)],
        compiler_params=_params("parallel", "arbitrary"),
        name="peer_dense",
    )(xn, u, vt, z, h2, g_final)


def _layer(h, mem2, tabs, g_attn, w_in, g_q_a, w_uq, g_kv_a, w_ukv, g_mla_o, g_moba_o, w_out, g_cross,
           g_mem, w_cq, w_ck, w_cv, w_co, g_ffn, w_pq, sub_keys1, sub_keys2, expert_u, expert_v,
           g_final, *, batch, seq, mem_len):
    D = h.shape[1]
    row = lambda g: g.reshape(1, -1)
    n_lat = MLA_Q_RANK + MLA_KV_RANK + MLA_ROPE
    w_in_p = jnp.concatenate([w_in[:, :n_lat], jnp.zeros((D, LANES - MLA_ROPE), w_in.dtype),
                              w_in[:, n_lat:]], axis=1).astype(BF16)
    w_uq_p = jnp.pad(w_uq.reshape(MLA_Q_RANK, MLA_HEADS, MLA_NOPE + MLA_ROPE),
                     ((0, 0), (0, 0), (0, MLA_QK_PAD - MLA_NOPE - MLA_ROPE))
                     ).reshape(MLA_Q_RANK, MLA_HEADS * MLA_QK_PAD).astype(BF16)
    w_ukv3 = w_ukv.reshape(MLA_KV_RANK, MLA_HEADS, MLA_NOPE + MLA_V)
    w_uk = w_ukv3[:, :, :MLA_NOPE].reshape(MLA_KV_RANK, MLA_HEADS * MLA_NOPE).astype(BF16)
    w_uv = w_ukv3[:, :, MLA_NOPE:].reshape(MLA_KV_RANK, MLA_HEADS * MLA_V).astype(BF16)

    ql, kl, vl, qm, km, vm, sel = _attn_prep(h, row(g_attn), w_in_p, row(g_q_a), w_uq_p, row(g_kv_a),
                                             w_uk, w_uv, tabs, seq=seq)
    o_mla = _mla_attention(ql, kl, vl, batch=batch, seq=seq)
    o_moba = _moba_attention(qm, km, vm, sel, batch=batch, seq=seq)
    h1 = _out_proj(h, o_mla, o_moba, row(g_mla_o), row(g_moba_o), w_out.astype(BF16))

    w_ckv = jnp.concatenate([w_ck, w_cv], axis=1).astype(BF16)
    kv = _rms_matmul(mem2, row(g_mem), w_ckv, BF16, tm=min(mem2.shape[0], 512), tn=1024)
    h2 = _cross_attention(h1, row(g_cross), w_cq.astype(BF16), kv, w_co.astype(BF16), seq=seq, mem_len=mem_len)

    xn, z = _peer_select(h2, row(g_ffn), w_pq.astype(BF16), sub_keys1, sub_keys2)
    vt = expert_v.reshape(-1, PEER_TE, D).transpose(0, 2, 1).astype(BF16)
    return _peer_dense(xn, expert_u.astype(BF16), vt, z, h2, row(g_final))


def kernel(x, mem, positions, g_attn, w_in, g_q_a, w_uq, g_kv_a, w_ukv, g_mla_o, g_moba_o, w_out, g_cross,
           g_mem, w_cq, w_ck, w_cv, w_co, g_ffn, w_pq, sub_keys1, sub_keys2, expert_u, expert_v, g_final):
    B, S, D = x.shape
    M = mem.shape[1]
    depth = w_in.shape[0]
    assert depth == 1, "the final norm is fused into the layer's last kernel"
    tabs = _rope_tables(positions.reshape(B * S, 1))
    out = _layer(x.reshape(B * S, D), mem.reshape(B * M, D), tabs, g_attn[0], w_in[0], g_q_a[0], w_uq[0],
                 g_kv_a[0], w_ukv[0], g_mla_o[0], g_moba_o[0], w_out[0], g_cross[0], g_mem[0], w_cq[0],
                 w_ck[0], w_cv[0], w_co[0], g_ffn[0], w_pq[0], sub_keys1[0], sub_keys2[0], expert_u[0],
                 expert_v[0], g_final, batch=B, seq=S, mem_len=M)
    return out.reshape(B, S, D)
```
